```python
import math
import jax, jax.numpy as jnp
from jax import lax
import numpy as np

D_MODEL = 2048
BATCH = 1
SEQ = 8192
DEPTH = 1

HEAD_DIM = 128
NA_HEADS = (D_MODEL // 2) // HEAD_DIM
ATTN_WIDTH = NA_HEADS * HEAD_DIM
CONV_WIDTH = D_MODEL - ATTN_WIDTH
CONV_GROUPS = CONV_WIDTH // HEAD_DIM
CONV_K = 3
GRID_W = 64
NA_KH_MAX = 8
NA_KW = 16
N_EXPERTS = 32
TOP_K = 4
D_FF = D_MODEL
SWIGLU_LIMIT = 7.0
SWIGLU_ALPHA = 1.702
MOE_BLOCK = 128
NORM_EPS = 1e-5
NEG_INF = -1e30

kernel_name = "hybrid_conv_natten_moe_encoder"


def rms_norm(x, w):
    xf = x.astype(jnp.float32)
    y = xf * lax.rsqrt(jnp.mean(xf * xf, axis=-1, keepdims=True) + NORM_EPS)
    return (y * w.astype(jnp.float32)).astype(x.dtype)


def group_rms_norm(x, w, group_size):
    shp = x.shape
    xf = x.astype(jnp.float32).reshape(shp[:-1] + (shp[-1] // group_size, group_size))
    y = xf * lax.rsqrt(jnp.mean(xf * xf, axis=-1, keepdims=True) + NORM_EPS)
    return (y.reshape(shp) * w.astype(jnp.float32)).astype(x.dtype)


def neighbourhood_attention(q, k, v, rel_pos_bias):
    bsz, s, _ = q.shape
    rows = s // GRID_W
    kh = min(NA_KH_MAX, rows)
    to_grid = lambda t: t.reshape(bsz, rows, GRID_W, NA_HEADS, HEAD_DIM)
    qg, kg, vg = to_grid(q), to_grid(k), to_grid(v)

    r = jnp.arange(rows)
    row_start = jnp.clip(r - kh // 2, 0, rows - kh)
    row_idx = row_start[:, None] + jnp.arange(kh)[None, :]
    k_rows = kg[:, row_idx]
    v_rows = vg[:, row_idx]

    c = jnp.arange(GRID_W)
    col_start = jnp.clip(c - NA_KW // 2, 0, GRID_W - NA_KW)
    col_in = (c[None, :] >= col_start[:, None]) & (c[None, :] < col_start[:, None] + NA_KW)

    dr = row_idx - r[:, None] + (NA_KH_MAX - 1)
    dc = jnp.clip(c[None, :] - c[:, None], -(NA_KW - 1), NA_KW - 1) + (NA_KW - 1)
    bias = rel_pos_bias[:, dr[:, None, :, None], dc[None, :, None, :]]

    scale = 1.0 / math.sqrt(HEAD_DIM)
    scores = jnp.einsum('brqhd,brikhd->bhrqik', qg, k_rows).astype(jnp.float32) * scale
    scores = scores + bias.astype(jnp.float32)[None]
    scores = jnp.where(col_in[:, None, :], scores, NEG_INF)
    p = jax.nn.softmax(scores, axis=(-2, -1))
    out = jnp.einsum('bhrqik,brikhd->brqhd', p.astype(v.dtype), v_rows)
    return out.reshape(bsz, s, ATTN_WIDTH)


def short_gated_conv(u, gate_b, gate_c, conv_w):
    s = u.shape[1]
    z = gate_c * u
    pad = CONV_K // 2
    zp = jnp.pad(z, ((0, 0), (pad, pad), (0, 0)))
    conv = sum(conv_w[j] * zp[:, j:j + s] for j in range(CONV_K))
    return gate_b * conv


def hybrid_mixer(h, w_in, conv_w, rel_pos_bias, attn_out_norm_w, conv_out_norm_w, w_out):
    proj = h @ w_in
    a, cw = ATTN_WIDTH, CONV_WIDTH
    q, k, v, u, gate_b, gate_c = jnp.split(
        proj, [a, 2 * a, 3 * a, 3 * a + cw, 3 * a + 2 * cw], axis=-1)
    attn = neighbourhood_attention(q, k, v, rel_pos_bias)
    conv = short_gated_conv(u, gate_b, gate_c, conv_w)
    mixed = jnp.concatenate([
        group_rms_norm(attn, attn_out_norm_w, HEAD_DIM),
        group_rms_norm(conv, conv_out_norm_w, CONV_WIDTH // CONV_GROUPS),
    ], axis=-1)
    return mixed @ w_out


def moe_ffn(h, router_w, router_b, w_gate_up, b_gate_up, w_down, b_down):
    t, d = h.shape
    logits = (h @ router_w + router_b).astype(jnp.float32)
    top_logits, top_idx = lax.top_k(logits, TOP_K)
    gates = jax.nn.softmax(top_logits, axis=-1).astype(h.dtype)

    n_assign = t * TOP_K
    flat_e = top_idx.reshape(-1).astype(jnp.int32)
    order = jnp.argsort(flat_e)
    sorted_e = flat_e[order]
    sorted_tok = (order // TOP_K).astype(jnp.int32)
    sorted_gate = gates.reshape(-1)[order]

    counts = jax.ops.segment_sum(jnp.ones_like(flat_e), flat_e, num_segments=N_EXPERTS)
    padded = (counts + MOE_BLOCK - 1) // MOE_BLOCK * MOE_BLOCK
    pad_end = jnp.cumsum(padded)
    pad_start = pad_end - padded
    start = jnp.cumsum(counts) - counts
    dest = pad_start[sorted_e] + jnp.arange(n_assign, dtype=jnp.int32) - start[sorted_e]

    n_blocks = -(-n_assign // MOE_BLOCK) + N_EXPERTS
    n_slots = n_blocks * MOE_BLOCK
    slot_tok = jnp.full((n_slots,), t, jnp.int32).at[dest].set(sorted_tok)
    slot_gate = jnp.zeros((n_slots,), h.dtype).at[dest].set(sorted_gate)
    block_expert = jnp.minimum(
        jnp.searchsorted(pad_end, jnp.arange(n_blocks, dtype=jnp.int32) * MOE_BLOCK, side='right'),
        N_EXPERTS - 1).astype(jnp.int32)

    h_pad = jnp.concatenate([h, jnp.zeros((1, d), h.dtype)], axis=0)
    xb = h_pad[slot_tok].reshape(n_blocks, MOE_BLOCK, d)

    def expert_block(args):
        xblk, e = args
        gu = xblk @ w_gate_up[e] + b_gate_up[e]
        g, lin = jnp.split(gu, 2, axis=-1)
        g = jnp.minimum(g, SWIGLU_LIMIT)
        lin = jnp.clip(lin, -SWIGLU_LIMIT, SWIGLU_LIMIT)
        act = g * jax.nn.sigmoid(SWIGLU_ALPHA * g) * (lin + 1.0)
        return act @ w_down[e] + b_down[e]

    yb = lax.map(expert_block, (xb, block_expert)).reshape(n_slots, d)
    out = jnp.zeros((t + 1, d), h.dtype).at[slot_tok].add(yb * slot_gate[:, None])
    return out[:t]


def setup_inputs(seed: int = 0) -> dict:
    key = jax.random.key(seed)
    ks = jax.random.split(key, 16)
    f32 = jnp.float32
    n_in = 3 * ATTN_WIDTH + 3 * CONV_WIDTH
    nrm = lambda k, shape, s: jax.random.normal(k, shape, f32) * s
    return {
        "x": nrm(ks[0], (BATCH, SEQ, D_MODEL), 1.0),
        "attn_norm_w": 1.0 + nrm(ks[1], (DEPTH, D_MODEL), 0.02),
        "w_in": nrm(ks[2], (DEPTH, D_MODEL, n_in), D_MODEL ** -0.5),
        "conv_w": nrm(ks[3], (DEPTH, CONV_K, CONV_WIDTH), CONV_K ** -0.5),
        "rel_pos_bias": nrm(ks[4], (DEPTH, NA_HEADS, 2 * NA_KH_MAX - 1, 2 * NA_KW - 1), 0.1),
        "attn_out_norm_w": 1.0 + nrm(ks[5], (DEPTH, ATTN_WIDTH), 0.02),
        "conv_out_norm_w": 1.0 + nrm(ks[6], (DEPTH, CONV_WIDTH), 0.02),
        "w_out": nrm(ks[7], (DEPTH, D_MODEL, D_MODEL), D_MODEL ** -0.5),
        "ffn_norm_w": 1.0 + nrm(ks[8], (DEPTH, D_MODEL), 0.02),
        "router_w": nrm(ks[9], (DEPTH, D_MODEL, N_EXPERTS), D_MODEL ** -0.5),
        "router_b": nrm(ks[10], (DEPTH, N_EXPERTS), 0.01),
        "w_gate_up": nrm(ks[11], (DEPTH, N_EXPERTS, D_MODEL, 2 * D_FF), D_MODEL ** -0.5),
        "b_gate_up": nrm(ks[12], (DEPTH, N_EXPERTS, 2 * D_FF), 0.02),
        "w_down": nrm(ks[13], (DEPTH, N_EXPERTS, D_FF, D_MODEL), D_FF ** -0.5),
        "b_down": nrm(ks[14], (DEPTH, N_EXPERTS, D_MODEL), 0.02),
        "final_norm_w": 1.0 + nrm(ks[15], (D_MODEL,), 0.02),
    }


def reference(x, attn_norm_w, w_in, conv_w, rel_pos_bias, attn_out_norm_w, conv_out_norm_w,
              w_out, ffn_norm_w, router_w, router_b, w_gate_up, b_gate_up, w_down, b_down,
              final_norm_w):
    bsz, s, d = x.shape
    h = x
    for layer in range(DEPTH):
        h = h + hybrid_mixer(rms_norm(h, attn_norm_w[layer]), w_in[layer], conv_w[layer],
                             rel_pos_bias[layer], attn_out_norm_w[layer],
                             conv_out_norm_w[layer], w_out[layer])
        f = moe_ffn(rms_norm(h, ffn_norm_w[layer]).reshape(bsz * s, d), router_w[layer],
                    router_b[layer], w_gate_up[layer], b_gate_up[layer], w_down[layer],
                    b_down[layer])
        h = h + f.reshape(bsz, s, d)
    return rms_norm(h, final_norm_w)
```

```python
import functools
import math

import jax
import jax.numpy as jnp
from jax import lax
from jax.experimental import pallas as pl
from jax.experimental.pallas import tpu as pltpu

F32 = jnp.float32
BF16 = jnp.bfloat16
I32 = jnp.int32
U32 = jnp.uint32

D_MODEL = 2048
SEQ = 8192
HEAD_DIM = 128
NA_HEADS = 8
ATTN_WIDTH = NA_HEADS * HEAD_DIM
CONV_WIDTH = D_MODEL - ATTN_WIDTH
CONV_GROUPS = CONV_WIDTH // HEAD_DIM
N_PROJ = 3 * ATTN_WIDTH + 3 * CONV_WIDTH
GRID_W = 64
GRID_ROWS = SEQ // GRID_W
NA_KH = 8
NA_KW = 16
N_EXPERTS = 32
TOP_K = 4
D_FF = D_MODEL
SWIGLU_LIMIT = 7.0
SWIGLU_ALPHA = 1.702
NORM_EPS = 1e-5
NEG_INF = -1e30

V7X_LANES = 128
V7X_VMEM_BYTES = 64 * 1024 * 1024
MIB = 1024 * 1024

INPROJ_TM = 1024
INPROJ_TN = 512
NORM_ROWS = 256
ATTN_ROWS_PER_STEP = 8
WIN_KEYS = NA_KH * GRID_W
CONV_CHUNK = 1024
OUT_TM = 1024
OUT_TN = 512
OUT_NCH = D_MODEL // OUT_TN
MOE_SUB = 256
MOE_TM = 5 * MOE_SUB
MOE_CH = 512
MOE_NCH = D_FF // MOE_CH
MOE_NB = (SEQ * TOP_K) // MOE_TM + 1 + N_EXPERTS - 1
N_SLOTS = MOE_NB * MOE_TM
HALF = D_MODEL // 2
DISPATCH_TB = 1024
COMBINE_TB = 256


def _cparams(semantics, vmem_mib):
    return pltpu.CompilerParams(dimension_semantics=semantics,
                                vmem_limit_bytes=vmem_mib * MIB)


def _inproj_body(x_ref, nw_ref, w_ref, o_ref, xn_ref):
    @pl.when(pl.program_id(1) == 0)
    def _():
        def chunk(i, carry):
            rows = pl.ds(pl.multiple_of(i * NORM_ROWS, NORM_ROWS), NORM_ROWS)
            x = x_ref[rows, :]
            ms = jnp.mean(x * x, axis=-1, keepdims=True)
            xn_ref[rows, :] = (x * lax.rsqrt(ms + NORM_EPS) * nw_ref[...]).astype(BF16)
            return carry
        lax.fori_loop(0, INPROJ_TM // NORM_ROWS, chunk, 0)

    o_ref[...] = jnp.dot(xn_ref[...], w_ref[...].astype(BF16),
                         preferred_element_type=F32).astype(BF16)


def _inproj(x2d, norm_w, w_in):
    return pl.pallas_call(
        _inproj_body,
        grid=(SEQ // INPROJ_TM, N_PROJ // INPROJ_TN),
        in_specs=[
            pl.BlockSpec((INPROJ_TM, D_MODEL), lambda m, n: (m, 0)),
            pl.BlockSpec((1, D_MODEL), lambda m, n: (0, 0)),
            pl.BlockSpec((D_MODEL, INPROJ_TN), lambda m, n: (0, n)),
        ],
        out_specs=pl.BlockSpec((INPROJ_TM, INPROJ_TN), lambda m, n: (m, n)),
        out_shape=jax.ShapeDtypeStruct((SEQ, N_PROJ), BF16),
        scratch_shapes=[pltpu.VMEM((INPROJ_TM, D_MODEL), BF16)],
        compiler_params=_cparams(("parallel", "arbitrary"), 48),
        name="inproj",
    )(x2d, norm_w, w_in)


def _bias_table(rel_pos_bias):
    cq = jnp.arange(GRID_W)[:, None]
    ck = jnp.arange(GRID_W)[None, :]
    cs = jnp.clip(cq - NA_KW // 2, 0, GRID_W - NA_KW)
    band = (ck >= cs) & (ck < cs + NA_KW)
    dc = jnp.clip(ck - cq, -(NA_KW - 1), NA_KW - 1) + (NA_KW - 1)
    t = jnp.where(band[None, None], rel_pos_bias[:, :, dc], NEG_INF)
    dr = jnp.arange(NA_KH)[:, None] + jnp.arange(NA_KH)[None, :]
    tb = t[:, dr]
    return tb.transpose(0, 1, 3, 2, 4).reshape(NA_HEADS, NA_KH, GRID_W, WIN_KEYS).astype(F32)


def _attn_body(q_ref, k_ref, v_ref, tb_ref, nw_ref, o_ref):
    rb = pl.program_id(1)
    scale = 1.0 / math.sqrt(HEAD_DIM)
    for i in range(ATTN_ROWS_PER_STEP):
        r = rb * ATTN_ROWS_PER_STEP + i
        ws = jnp.clip(r - NA_KH // 2, 0, GRID_ROWS - NA_KH)
        off = ws - r + (NA_KH - 1)
        kstart = pl.multiple_of(ws * GRID_W, GRID_W)
        kk = k_ref[pl.ds(kstart, WIN_KEYS), :]
        vv = v_ref[pl.ds(kstart, WIN_KEYS), :]
        q = q_ref[i * GRID_W:(i + 1) * GRID_W, :]
        s = lax.dot_general(q, kk, (((1,), (1,)), ((), ())), preferred_element_type=F32)
        s = s * scale + tb_ref[0, off]
        m = jnp.max(s, axis=-1, keepdims=True)
        p = jnp.exp(s - m)
        l = jnp.sum(p, axis=-1, keepdims=True)
        o = jnp.dot(p.astype(BF16), vv, preferred_element_type=F32) / l
        ms = jnp.mean(o * o, axis=-1, keepdims=True)
        o_ref[i * GRID_W:(i + 1) * GRID_W, :] = (
            o * lax.rsqrt(ms + NORM_EPS) * nw_ref[...]).astype(BF16)


def _attention(proj, bias_tab, attn_out_norm_w):
    rows = ATTN_ROWS_PER_STEP * GRID_W
    return pl.pallas_call(
        _attn_body,
        grid=(NA_HEADS, GRID_ROWS // ATTN_ROWS_PER_STEP),
        in_specs=[
            pl.BlockSpec((rows, HEAD_DIM), lambda h, rb: (rb, h)),
            pl.BlockSpec((SEQ, HEAD_DIM), lambda h, rb: (0, NA_HEADS + h)),
            pl.BlockSpec((SEQ, HEAD_DIM), lambda h, rb: (0, 2 * NA_HEADS + h)),
            pl.BlockSpec((1, NA_KH, GRID_W, WIN_KEYS), lambda h, rb: (h, 0, 0, 0)),
            pl.BlockSpec((1, HEAD_DIM), lambda h, rb: (0, h)),
        ],
        out_specs=pl.BlockSpec((rows, HEAD_DIM), lambda h, rb: (rb, h)),
        out_shape=jax.ShapeDtypeStruct((SEQ, ATTN_WIDTH), BF16),
        compiler_params=_cparams(("parallel", "arbitrary"), 32),
        name="attn",
    )(proj, proj, proj, bias_tab, attn_out_norm_w)


CONV_PAD = 8


def _conv_body(u_ref, b_ref, c_ref, cw_ref, nw_ref, o_ref, z_ref):
    zeros = jnp.zeros((CONV_PAD, HEAD_DIM), F32)
    z_ref[0:CONV_PAD, :] = zeros
    z_ref[CONV_PAD + SEQ:2 * CONV_PAD + SEQ, :] = zeros
    for ch in range(SEQ // CONV_CHUNK):
        lo = ch * CONV_CHUNK
        z_ref[CONV_PAD + lo:CONV_PAD + lo + CONV_CHUNK, :] = (
            c_ref[lo:lo + CONV_CHUNK, :].astype(F32) * u_ref[lo:lo + CONV_CHUNK, :].astype(F32))
    w0 = cw_ref[0:1, :]
    w1 = cw_ref[1:2, :]
    w2 = cw_ref[2:3, :]
    for ch in range(SEQ // CONV_CHUNK):
        lo = ch * CONV_CHUNK
        base = CONV_PAD + lo
        conv = (w0 * z_ref[base - 1:base - 1 + CONV_CHUNK, :]
                + w1 * z_ref[base:base + CONV_CHUNK, :]
                + w2 * z_ref[base + 1:base + 1 + CONV_CHUNK, :])
        y = b_ref[lo:lo + CONV_CHUNK, :].astype(F32) * conv
        ms = jnp.mean(y * y, axis=-1, keepdims=True)
        o_ref[lo:lo + CONV_CHUNK, :] = (y * lax.rsqrt(ms + NORM_EPS) * nw_ref[...]).astype(BF16)


def _conv(proj, conv_w, conv_out_norm_w):
    col0 = 3 * NA_HEADS
    return pl.pallas_call(
        _conv_body,
        grid=(CONV_GROUPS,),
        in_specs=[
            pl.BlockSpec((SEQ, HEAD_DIM), lambda g: (0, col0 + g)),
            pl.BlockSpec((SEQ, HEAD_DIM), lambda g: (0, col0 + CONV_GROUPS + g)),
            pl.BlockSpec((SEQ, HEAD_DIM), lambda g: (0, col0 + 2 * CONV_GROUPS + g)),
            pl.BlockSpec((3, HEAD_DIM), lambda g: (0, g)),
            pl.BlockSpec((1, HEAD_DIM), lambda g: (0, g)),
        ],
        out_specs=pl.BlockSpec((SEQ, HEAD_DIM), lambda g: (0, g)),
        out_shape=jax.ShapeDtypeStruct((SEQ, CONV_WIDTH), BF16),
        scratch_shapes=[pltpu.VMEM((SEQ + 2 * CONV_PAD, HEAD_DIM), F32)],
        compiler_params=_cparams(("arbitrary",), 48),
        name="conv",
    )(proj, proj, proj, conv_w, conv_out_norm_w)


def _bf16_bits(x):
    return lax.bitcast_convert_type(x.astype(BF16).astype(F32), U32) & jnp.uint32(0xFFFF0000)


def _outproj_body(a_ref, c_ref, w_ref, x_ref, fw_ref, rwt_ref, rb_ref,
                  h_ref, hp_ref, ids_ref, gates_ref, ranks_ref, cnt_ref,
                  hacc_ref, carry_ref, tri_ref):
    m = pl.program_id(0)
    n = pl.program_id(1)

    @pl.when((m == 0) & (n == 0))
    def _():
        carry_ref[...] = jnp.zeros_like(carry_ref)
        r = lax.broadcasted_iota(I32, (OUT_TM, OUT_TM), 0)
        c = lax.broadcasted_iota(I32, (OUT_TM, OUT_TM), 1)
        tri_ref[...] = (r < c).astype(BF16)

    w = w_ref[...].astype(BF16)
    acc = (jnp.dot(a_ref[...], w[:ATTN_WIDTH], preferred_element_type=F32)
           + jnp.dot(c_ref[...], w[ATTN_WIDTH:], preferred_element_type=F32)
           + x_ref[...])
    h_ref[...] = acc
    hacc_ref[n] = acc

    @pl.when(n == OUT_NCH - 1)
    def _():
        ss = jnp.zeros((OUT_TM, 1), F32)
        for j in range(OUT_NCH):
            hj = hacc_ref[j]
            ss = ss + jnp.sum(hj * hj, axis=-1, keepdims=True)
        inv = lax.rsqrt(ss * (1.0 / D_MODEL) + NORM_EPS)
        hn = []
        logits = jnp.zeros((N_EXPERTS, OUT_TM), F32)
        for j in range(OUT_NCH):
            hj = (hacc_ref[j] * inv * fw_ref[:, j * OUT_TN:(j + 1) * OUT_TN]).astype(BF16)
            hn.append(hj)
            logits = logits + lax.dot_general(
                rwt_ref[:, j * OUT_TN:(j + 1) * OUT_TN], hj,
                (((1,), (1,)), ((), ())), preferred_element_type=F32)
        logits = logits + rb_ref[:, 0:1]
        half_chunks = OUT_NCH // 2
        for j in range(half_chunks):
            lo = _bf16_bits(hn[j].astype(F32)) >> 16
            hi = _bf16_bits(hn[j + half_chunks].astype(F32))
            hp_ref[:, j * OUT_TN:(j + 1) * OUT_TN] = lo | hi

        eio = lax.broadcasted_iota(I32, (N_EXPERTS, OUT_TM), 0)
        work = logits
        tops, idxs, hots = [], [], []
        for _ in range(TOP_K):
            mx = jnp.max(work, axis=0, keepdims=True)
            idx = jnp.min(jnp.where(work == mx, eio, N_EXPERTS), axis=0, keepdims=True)
            hot = eio == idx
            tops.append(mx)
            idxs.append(idx)
            hots.append(hot)
            work = jnp.where(hot, -jnp.inf, work)
        exps = [jnp.exp(t - tops[0]) for t in tops]
        den = exps[0] + exps[1] + exps[2] + exps[3]
        multi = jnp.zeros((N_EXPERTS, OUT_TM), F32)
        for hot in hots:
            multi = multi + hot.astype(F32)
        carry = carry_ref[:, 0:1]
        before = carry + jnp.dot(multi.astype(BF16), tri_ref[...], preferred_element_type=F32)
        zero_i = jnp.zeros((1, OUT_TM), I32)
        zero_f = jnp.zeros((1, OUT_TM), F32)
        for k in range(TOP_K):
            ids_ref[k:k + 1, :] = idxs[k]
            gates_ref[k:k + 1, :] = exps[k] / den
            rank = jnp.sum(jnp.where(hots[k], before, 0.0), axis=0, keepdims=True)
            ranks_ref[k:k + 1, :] = rank.astype(I32)
            ids_ref[TOP_K + k:TOP_K + k + 1, :] = zero_i
            gates_ref[TOP_K + k:TOP_K + k + 1, :] = zero_f
            ranks_ref[TOP_K + k:TOP_K + k + 1, :] = zero_i
        new_carry = carry + jnp.sum(multi, axis=1, keepdims=True)
        full = jnp.broadcast_to(new_carry, (N_EXPERTS, V7X_LANES))
        carry_ref[...] = full
        cnt_ref[...] = full


def _outproj_router(attn_n, conv_n, w_out, x2d, ffn_norm_w, router_wt, router_b_col):
    grid = (SEQ // OUT_TM, OUT_NCH)
    meta = lambda m, n: (0, m)
    return pl.pallas_call(
        _outproj_body,
        grid=grid,
        in_specs=[
            pl.BlockSpec((OUT_TM, ATTN_WIDTH), lambda m, n: (m, 0)),
            pl.BlockSpec((OUT_TM, CONV_WIDTH), lambda m, n: (m, 0)),
            pl.BlockSpec((D_MODEL, OUT_TN), lambda m, n: (0, n)),
            pl.BlockSpec((OUT_TM, OUT_TN), lambda m, n: (m, n)),
            pl.BlockSpec((1, D_MODEL), lambda m, n: (0, 0)),
            pl.BlockSpec((N_EXPERTS, D_MODEL), lambda m, n: (0, 0)),
            pl.BlockSpec((N_EXPERTS, V7X_LANES), lambda m, n: (0, 0)),
        ],
        out_specs=[
            pl.BlockSpec((OUT_TM, OUT_TN), lambda m, n: (m, n)),
            pl.BlockSpec((OUT_TM, HALF), lambda m, n: (m, 0)),
            pl.BlockSpec((2 * TOP_K, OUT_TM), meta),
            pl.BlockSpec((2 * TOP_K, OUT_TM), meta),
            pl.BlockSpec((2 * TOP_K, OUT_TM), meta),
            pl.BlockSpec((N_EXPERTS, V7X_LANES), lambda m, n: (0, 0)),
        ],
        out_shape=[
            jax.ShapeDtypeStruct((SEQ, D_MODEL), F32),
            jax.ShapeDtypeStruct((SEQ, HALF), U32),
            jax.ShapeDtypeStruct((2 * TOP_K, SEQ), I32),
            jax.ShapeDtypeStruct((2 * TOP_K, SEQ), F32),
            jax.ShapeDtypeStruct((2 * TOP_K, SEQ), I32),
            jax.ShapeDtypeStruct((N_EXPERTS, V7X_LANES), F32),
        ],
        scratch_shapes=[
            pltpu.VMEM((OUT_NCH, OUT_TM, OUT_TN), F32),
            pltpu.VMEM((N_EXPERTS, V7X_LANES), F32),
            pltpu.VMEM((OUT_TM, OUT_TM), BF16),
        ],
        compiler_params=_cparams(("arbitrary", "arbitrary"), 56),
        name="outproj_router",
    )(attn_n, conv_n, w_out, x2d, ffn_norm_w, router_wt, router_b_col)


HBM_ROW_TILE = 8
PAD_BITS = (128, 64, 32, 16, 8)


def _dispatch_body(dest_ref, pad_start_ref, pad_len_ref, hp_ref, xs_ref, zbuf_ref, sem, zsem):
    i = pl.program_id(0)

    def row_copy(tok, d):
        return pltpu.make_async_copy(hp_ref.at[pl.ds(tok, 1)], xs_ref.at[pl.ds(d, 1)], sem)

    def pad_head(e):
        return (-pad_start_ref[e]) & (HBM_ROW_TILE - 1)

    def pad_row_copy(e, j):
        return pltpu.make_async_copy(zbuf_ref.at[pl.ds(0, 1)],
                                     xs_ref.at[pl.ds(pad_start_ref[e] + j, 1)], zsem)

    def pad_piece_copy(e, bit):
        body = pad_len_ref[e] - pad_head(e)
        start = pad_start_ref[e] + pad_head(e) + (body & ~(2 * bit - 1))
        start = pl.multiple_of(start, HBM_ROW_TILE)
        return pltpu.make_async_copy(zbuf_ref.at[pl.ds(0, bit)], xs_ref.at[pl.ds(start, bit)], zsem)

    def for_each_pad(action):
        def per_expert(e, carry):
            for j in range(HBM_ROW_TILE - 1):
                @pl.when(j < pad_head(e))
                def _():
                    action(pad_row_copy(e, j))
            for bit in PAD_BITS:
                @pl.when(((pad_len_ref[e] - pad_head(e)) & bit) != 0)
                def _():
                    action(pad_piece_copy(e, bit))
            return carry
        lax.fori_loop(0, N_EXPERTS, per_expert, 0)

    @pl.when(i == 0)
    def _():
        zbuf_ref[...] = jnp.zeros_like(zbuf_ref)
        for_each_pad(lambda cp: cp.start())
        for_each_pad(lambda cp: cp.wait())

    def start_rows(t, carry):
        tok = i * DISPATCH_TB + t
        for k in range(TOP_K):
            row_copy(tok, dest_ref[k * SEQ + tok]).start()
        return carry
    lax.fori_loop(0, DISPATCH_TB, start_rows, 0)

    def wait_rows(t, carry):
        tok = i * DISPATCH_TB + t
        for k in range(TOP_K):
            row_copy(tok, dest_ref[k * SEQ + tok]).wait()
        return carry
    lax.fori_loop(0, DISPATCH_TB, wait_rows, 0)


def _dispatch(dest_flat, pad_start, pad_len, hp):
    grid_spec = pltpu.PrefetchScalarGridSpec(
        num_scalar_prefetch=3,
        grid=(SEQ // DISPATCH_TB,),
        in_specs=[pl.BlockSpec(memory_space=pl.ANY)],
        out_specs=pl.BlockSpec(memory_space=pl.ANY),
        scratch_shapes=[
            pltpu.VMEM((PAD_BITS[0], HALF), U32),
            pltpu.SemaphoreType.DMA(()),
            pltpu.SemaphoreType.DMA(()),
        ],
    )
    return pl.pallas_call(
        _dispatch_body,
        grid_spec=grid_spec,
        out_shape=jax.ShapeDtypeStruct((N_SLOTS, HALF), U32),
        compiler_params=pltpu.CompilerParams(dimension_semantics=("arbitrary",),
                                             has_side_effects=True),
        name="dispatch",
    )(dest_flat, pad_start, pad_len, hp)


def _unpack_rows(words):
    lo = lax.bitcast_convert_type(words << 16, F32).astype(BF16)
    hi = lax.bitcast_convert_type(words & jnp.uint32(0xFFFF0000), F32).astype(BF16)
    return lo, hi


def _cast_weight(src_ref, dst_ref, slot):
    def chunk(i, carry):
        rows = pl.ds(pl.multiple_of(i * MOE_SUB, MOE_SUB), MOE_SUB)
        dst_ref[slot, rows, :] = src_ref[0, rows, :].astype(BF16)
        return carry
    lax.fori_loop(0, D_MODEL // MOE_SUB, chunk, 0)


def _moe_body(be_ref, bx_ref, ns_ref, nu_ref,
              x_ref, wg_ref, wl_ref, wd_ref, bg_ref, bl_ref, bd_ref,
              o_ref, wbf_ref, act_ref):
    b = pl.program_id(0)
    c = pl.program_id(1)
    valid = b < nu_ref[0]
    nsub = ns_ref[b]

    @pl.when(valid & (c < MOE_NCH))
    def _():
        _cast_weight(wg_ref, wbf_ref, 0)
        _cast_weight(wl_ref, wbf_ref, 1)
        bg = bg_ref[0]
        bl = bl_ref[0]

        def sub(i, carry):
            rows = pl.ds(pl.multiple_of(i * MOE_SUB, MOE_SUB), MOE_SUB)
            lo, hi = _unpack_rows(x_ref[rows, :])
            g = (jnp.dot(lo, wbf_ref[0, :HALF, :], preferred_element_type=F32)
                 + jnp.dot(hi, wbf_ref[0, HALF:, :], preferred_element_type=F32) + bg)
            lin = (jnp.dot(lo, wbf_ref[1, :HALF, :], preferred_element_type=F32)
                   + jnp.dot(hi, wbf_ref[1, HALF:, :], preferred_element_type=F32) + bl)
            g = jnp.minimum(g, SWIGLU_LIMIT)
            lin = jnp.clip(lin, -SWIGLU_LIMIT, SWIGLU_LIMIT)
            act = g * jax.nn.sigmoid(SWIGLU_ALPHA * g) * (lin + 1.0)
            act_ref[c, rows, :] = act.astype(BF16)
            return carry
        lax.fori_loop(0, nsub, sub, 0)

    @pl.when(valid & (c >= MOE_NCH))
    def _():
        _cast_weight(wd_ref, wbf_ref, 0)
        bd = bd_ref[0]

        def sub(i, carry):
            rows = pl.ds(pl.multiple_of(i * MOE_SUB, MOE_SUB), MOE_SUB)
            y = jnp.zeros((MOE_SUB, MOE_CH), F32) + bd
            for k in range(MOE_NCH):
                y = y + jnp.dot(act_ref[k, rows, :], wbf_ref[0, k * MOE_CH:(k + 1) * MOE_CH, :],
                                preferred_element_type=F32)
            o_ref[rows, :] = y
            return carry
        lax.fori_loop(0, nsub, sub, 0)

        def fill(i, carry):
            rows = pl.ds(pl.multiple_of(i * MOE_SUB, MOE_SUB), MOE_SUB)
            o_ref[rows, :] = jnp.zeros((MOE_SUB, MOE_CH), F32)
            return carry
        lax.fori_loop(nsub, MOE_TM // MOE_SUB, fill, 0)


def _moe(blk_expert, blk_x, blk_nsub, n_used, xs, w_gate_up, w_down, b_gate_up, b_down):
    last = MOE_NCH - 1

    def up_chunk(b, c, nu):
        return jnp.where(b < nu[0], jnp.minimum(c, last), last)

    def down_chunk(b, c, nu):
        return jnp.where(b < nu[0], jnp.maximum(c - MOE_NCH, 0), last)

    grid_spec = pltpu.PrefetchScalarGridSpec(
        num_scalar_prefetch=4,
        grid=(MOE_NB, 2 * MOE_NCH),
        in_specs=[
            pl.BlockSpec((MOE_TM, HALF), lambda b, c, be, bx, ns, nu: (bx[b], 0)),
            pl.BlockSpec((1, D_MODEL, MOE_CH),
                         lambda b, c, be, bx, ns, nu: (be[b], 0, up_chunk(b, c, nu))),
            pl.BlockSpec((1, D_MODEL, MOE_CH),
                         lambda b, c, be, bx, ns, nu: (be[b], 0, MOE_NCH + up_chunk(b, c, nu))),
            pl.BlockSpec((1, D_FF, MOE_CH),
                         lambda b, c, be, bx, ns, nu: (be[b], 0, down_chunk(b, c, nu))),
            pl.BlockSpec((1, 1, MOE_CH),
                         lambda b, c, be, bx, ns, nu: (be[b], 0, up_chunk(b, c, nu))),
            pl.BlockSpec((1, 1, MOE_CH),
                         lambda b, c, be, bx, ns, nu: (be[b], 0, MOE_NCH + up_chunk(b, c, nu))),
            pl.BlockSpec((1, 1, MOE_CH),
                         lambda b, c, be, bx, ns, nu: (be[b], 0, down_chunk(b, c, nu))),
        ],
        out_specs=pl.BlockSpec((MOE_TM, MOE_CH),
                               lambda b, c, be, bx, ns, nu: (bx[b], down_chunk(b, c, nu))),
        scratch_shapes=[
            pltpu.VMEM((2, D_MODEL, MOE_CH), BF16),
            pltpu.VMEM((MOE_NCH, MOE_TM, MOE_CH), BF16),
        ],
    )
    return pl.pallas_call(
        _moe_body,
        grid_spec=grid_spec,
        out_shape=jax.ShapeDtypeStruct((N_SLOTS, D_MODEL), F32),
        compiler_params=_cparams(("arbitrary", "arbitrary"), 60),
        name="moe",
    )(blk_expert, blk_x, blk_nsub, n_used, xs, w_gate_up, w_gate_up, w_down,
      b_gate_up, b_gate_up, b_down)


def _combine_body(dest_ref, gates_ref, h_ref, nw_ref, ys_ref, o_ref, buf_ref, sem):
    i = pl.program_id(0)
    nblk = pl.num_programs(0)

    def row_copy(blk, slot, k, t):
        d = dest_ref[k * SEQ + blk * COMBINE_TB + t]
        return pltpu.make_async_copy(ys_ref.at[pl.ds(d, 1)],
                                     buf_ref.at[slot, k, pl.ds(t, 1)], sem.at[slot])

    def start_block(blk, slot):
        def body(t, carry):
            for k in range(TOP_K):
                row_copy(blk, slot, k, t).start()
            return carry
        lax.fori_loop(0, COMBINE_TB, body, 0)

    def wait_block(blk, slot):
        def body(t, carry):
            for k in range(TOP_K):
                row_copy(blk, slot, k, t).wait()
            return carry
        lax.fori_loop(0, COMBINE_TB, body, 0)

    slot = i % 2

    @pl.when(i == 0)
    def _():
        start_block(0, 0)

    @pl.when(i + 1 < nblk)
    def _():
        start_block(i + 1, 1 - slot)

    wait_block(i, slot)

    acc = h_ref[...]
    for k in range(TOP_K):
        acc = acc + gates_ref[:, k:k + 1] * buf_ref[slot, k]
    ms = jnp.mean(acc * acc, axis=-1, keepdims=True)
    o_ref[...] = acc * lax.rsqrt(ms + NORM_EPS) * nw_ref[...]


def _combine(dest_flat, gates_t, h, final_norm_w, ys):
    grid_spec = pltpu.PrefetchScalarGridSpec(
        num_scalar_prefetch=1,
        grid=(SEQ // COMBINE_TB,),
        in_specs=[
            pl.BlockSpec((COMBINE_TB, TOP_K), lambda i, d: (i, 0)),
            pl.BlockSpec((COMBINE_TB, D_MODEL), lambda i, d: (i, 0)),
            pl.BlockSpec((1, D_MODEL), lambda i, d: (0, 0)),
            pl.BlockSpec(memory_space=pl.ANY),
        ],
        out_specs=pl.BlockSpec((COMBINE_TB, D_MODEL), lambda i, d: (i, 0)),
        scratch_shapes=[
            pltpu.VMEM((2, TOP_K, COMBINE_TB, D_MODEL), F32),
            pltpu.SemaphoreType.DMA((2,)),
        ],
    )
    return pl.pallas_call(
        _combine_body,
        grid_spec=grid_spec,
        out_shape=jax.ShapeDtypeStruct((SEQ, D_MODEL), F32),
        compiler_params=_cparams(("arbitrary",), 40),
        name="combine",
    )(dest_flat, gates_t, h, final_norm_w, ys)


def _routing_tables(counts, ids, ranks):
    nblk = (counts + MOE_TM - 1) // MOE_TM
    cum = jnp.cumsum(nblk)
    first_blk = cum - nblk
    n_used = cum[-1]
    dest = first_blk[ids] * MOE_TM + ranks
    b = jnp.arange(MOE_NB, dtype=I32)
    valid = b < n_used
    blk_e = jnp.minimum(jnp.searchsorted(cum, b, side="right"), N_EXPERTS - 1).astype(I32)
    blk_e = jnp.where(valid, blk_e, blk_e[n_used - 1])
    rows = jnp.clip(counts[blk_e] - (b - first_blk[blk_e]) * MOE_TM, 0, MOE_TM)
    blk_nsub = jnp.where(valid, (rows + MOE_SUB - 1) // MOE_SUB, 0).astype(I32)
    blk_x = jnp.where(valid, b, n_used - 1).astype(I32)
    pad_start = (first_blk * MOE_TM + counts).astype(I32)
    pad_len = ((-counts) % MOE_SUB).astype(I32)
    return (dest.reshape(-1).astype(I32), blk_e, blk_x, blk_nsub,
            n_used.reshape(1).astype(I32), pad_start, pad_len)


def kernel(x, attn_norm_w, w_in, conv_w, rel_pos_bias, attn_out_norm_w, conv_out_norm_w, w_out,
           ffn_norm_w, router_w, router_b, w_gate_up, b_gate_up, w_down, b_down, final_norm_w):
    bsz, s, d = x.shape
    assert (bsz, s, d) == (1, SEQ, D_MODEL)
    assert attn_norm_w.shape[0] == 1
    x2d = x.reshape(SEQ, D_MODEL)

    proj = _inproj(x2d, attn_norm_w[0].reshape(1, D_MODEL), w_in[0])
    attn_n = _attention(proj, _bias_table(rel_pos_bias[0]), attn_out_norm_w[0].reshape(1, ATTN_WIDTH))
    conv_n = _conv(proj, conv_w[0], conv_out_norm_w[0].reshape(1, CONV_WIDTH))

    router_wt = router_w[0].T.astype(BF16)
    router_b_col = jnp.broadcast_to(router_b[0][:, None], (N_EXPERTS, V7X_LANES))
    h, hp, ids, gates, ranks, cnt = _outproj_router(
        attn_n, conv_n, w_out[0], x2d, ffn_norm_w[0].reshape(1, D_MODEL), router_wt, router_b_col)

    counts = cnt[:, 0].astype(I32)
    dest, blk_e, blk_x, blk_nsub, n_used, pad_start, pad_len = _routing_tables(
        counts, ids[:TOP_K], ranks[:TOP_K])

    xs = _dispatch(dest, pad_start, pad_len, hp)
    ys = _moe(blk_e, blk_x, blk_nsub, n_used, xs, w_gate_up[0], w_down[0],
              b_gate_up[0].reshape(N_EXPERTS, 1, 2 * D_FF), b_down[0].reshape(N_EXPERTS, 1, D_MODEL))
    out = _combine(dest, gates[:TOP_K].T, h, final_norm_w.reshape(1, D_MODEL), ys)
    return out.reshape(bsz, s, d)
```

```python
import functools
import math

import jax
import jax.numpy as jnp
from jax import lax
from jax.experimental import pallas as pl
from jax.experimental.pallas import tpu as pltpu

F32 = jnp.float32
BF16 = jnp.bfloat16
I32 = jnp.int32
U32 = jnp.uint32

D_MODEL = 2048
SEQ = 8192
HEAD_DIM = 128
NA_HEADS = 8
ATTN_WIDTH = NA_HEADS * HEAD_DIM
CONV_WIDTH = D_MODEL - ATTN_WIDTH
CONV_GROUPS = CONV_WIDTH // HEAD_DIM
N_PROJ = 3 * ATTN_WIDTH + 3 * CONV_WIDTH
GRID_W = 64
GRID_ROWS = SEQ // GRID_W
NA_KH = 8
NA_KW = 16
N_EXPERTS = 32
TOP_K = 4
D_FF = D_MODEL
SWIGLU_LIMIT = 7.0
SWIGLU_ALPHA = 1.702
NORM_EPS = 1e-5
NEG_INF = -1e30

V7X_LANES = 128
V7X_VMEM_BYTES = 64 * 1024 * 1024
MIB = 1024 * 1024

INPROJ_TM = 1024
INPROJ_TN = 512
NORM_ROWS = 256
ATTN_ROWS_PER_STEP = 8
WIN_KEYS = NA_KH * GRID_W
CONV_CHUNK = 1024
OUT_TM = 1024
OUT_TN = 512
OUT_NCH = D_MODEL // OUT_TN
MOE_SUB = 256
MOE_TM = 5 * MOE_SUB
MOE_CH = 512
MOE_NCH = D_FF // MOE_CH
MOE_NB = (SEQ * TOP_K) // MOE_TM + 1 + N_EXPERTS - 1
N_SLOTS = MOE_NB * MOE_TM
HALF = D_MODEL // 2
DISPATCH_TB = 1024
COMBINE_TB = 256


def _cparams(semantics, vmem_mib):
    return pltpu.CompilerParams(dimension_semantics=semantics,
                                vmem_limit_bytes=vmem_mib * MIB)


def _inproj_body(x_ref, nw_ref, w_ref, o_ref, xn_ref):
    @pl.when(pl.program_id(1) == 0)
    def _():
        def chunk(i, carry):
            rows = pl.ds(pl.multiple_of(i * NORM_ROWS, NORM_ROWS), NORM_ROWS)
            x = x_ref[rows, :]
            ms = jnp.mean(x * x, axis=-1, keepdims=True)
            xn_ref[rows, :] = (x * lax.rsqrt(ms + NORM_EPS) * nw_ref[...]).astype(BF16)
            return carry
        lax.fori_loop(0, INPROJ_TM // NORM_ROWS, chunk, 0)

    o_ref[...] = jnp.dot(xn_ref[...], w_ref[...].astype(BF16),
                         preferred_element_type=F32).astype(BF16)


def _inproj(x2d, norm_w, w_in):
    return pl.pallas_call(
        _inproj_body,
        grid=(SEQ // INPROJ_TM, N_PROJ // INPROJ_TN),
        in_specs=[
            pl.BlockSpec((INPROJ_TM, D_MODEL), lambda m, n: (m, 0)),
            pl.BlockSpec((1, D_MODEL), lambda m, n: (0, 0)),
            pl.BlockSpec((D_MODEL, INPROJ_TN), lambda m, n: (0, n)),
        ],
        out_specs=pl.BlockSpec((INPROJ_TM, INPROJ_TN), lambda m, n: (m, n)),
        out_shape=jax.ShapeDtypeStruct((SEQ, N_PROJ), BF16),
        scratch_shapes=[pltpu.VMEM((INPROJ_TM, D_MODEL), BF16)],
        compiler_params=_cparams(("parallel", "arbitrary"), 48),
        name="inproj",
    )(x2d, norm_w, w_in)


def _bias_table(rel_pos_bias):
    cq = jnp.arange(GRID_W)[:, None]
    ck = jnp.arange(GRID_W)[None, :]
    cs = jnp.clip(cq - NA_KW // 2, 0, GRID_W - NA_KW)
    band = (ck >= cs) & (ck < cs + NA_KW)
    dc = jnp.clip(ck - cq, -(NA_KW - 1), NA_KW - 1) + (NA_KW - 1)
    t = jnp.zeros((NA_HEADS, 2 * NA_KH - 1, GRID_W, GRID_W), F32)
    for j in range(2 * NA_KW - 1):
        t = t + jnp.where(dc == j, rel_pos_bias[:, :, j, None, None], 0.0)
    t = jnp.where(band[None, None], t, NEG_INF)
    tb = jnp.stack([t[:, off:off + NA_KH] for off in range(NA_KH)], axis=1)
    return tb.transpose(0, 1, 3, 2, 4).reshape(NA_HEADS, NA_KH, GRID_W, WIN_KEYS).astype(F32)


def _attn_body(q_ref, k_ref, v_ref, tb_ref, nw_ref, o_ref):
    rb = pl.program_id(1)
    scale = 1.0 / math.sqrt(HEAD_DIM)
    for i in range(ATTN_ROWS_PER_STEP):
        r = rb * ATTN_ROWS_PER_STEP + i
        ws = jnp.clip(r - NA_KH // 2, 0, GRID_ROWS - NA_KH)
        off = ws - r + (NA_KH - 1)
        kstart = pl.multiple_of(ws * GRID_W, GRID_W)
        kk = k_ref[pl.ds(kstart, WIN_KEYS), :]
        vv = v_ref[pl.ds(kstart, WIN_KEYS), :]
        q = q_ref[i * GRID_W:(i + 1) * GRID_W, :]
        s = lax.dot_general(q, kk, (((1,), (1,)), ((), ())), preferred_element_type=F32)
        s = s * scale + tb_ref[0, off]
        m = jnp.max(s, axis=-1, keepdims=True)
        p = jnp.exp(s - m)
        l = jnp.sum(p, axis=-1, keepdims=True)
        o = jnp.dot(p.astype(BF16), vv, preferred_element_type=F32) / l
        ms = jnp.mean(o * o, axis=-1, keepdims=True)
        o_ref[i * GRID_W:(i + 1) * GRID_W, :] = (
            o * lax.rsqrt(ms + NORM_EPS) * nw_ref[...]).astype(BF16)


def _attention(proj, bias_tab, attn_out_norm_w):
    rows = ATTN_ROWS_PER_STEP * GRID_W
    return pl.pallas_call(
        _attn_body,
        grid=(NA_HEADS, GRID_ROWS // ATTN_ROWS_PER_STEP),
        in_specs=[
            pl.BlockSpec((rows, HEAD_DIM), lambda h, rb: (rb, h)),
            pl.BlockSpec((SEQ, HEAD_DIM), lambda h, rb: (0, NA_HEADS + h)),
            pl.BlockSpec((SEQ, HEAD_DIM), lambda h, rb: (0, 2 * NA_HEADS + h)),
            pl.BlockSpec((1, NA_KH, GRID_W, WIN_KEYS), lambda h, rb: (h, 0, 0, 0)),
            pl.BlockSpec((1, HEAD_DIM), lambda h, rb: (0, h)),
        ],
        out_specs=pl.BlockSpec((rows, HEAD_DIM), lambda h, rb: (rb, h)),
        out_shape=jax.ShapeDtypeStruct((SEQ, ATTN_WIDTH), BF16),
        compiler_params=_cparams(("parallel", "arbitrary"), 32),
        name="attn",
    )(proj, proj, proj, bias_tab, attn_out_norm_w)


CONV_PAD = 8


def _conv_body(u_ref, b_ref, c_ref, cw_ref, nw_ref, o_ref, z_ref):
    zeros = jnp.zeros((CONV_PAD, HEAD_DIM), F32)
    z_ref[0:CONV_PAD, :] = zeros
    z_ref[CONV_PAD + SEQ:2 * CONV_PAD + SEQ, :] = zeros
    for ch in range(SEQ // CONV_CHUNK):
        lo = ch * CONV_CHUNK
        z_ref[CONV_PAD + lo:CONV_PAD + lo + CONV_CHUNK, :] = (
            c_ref[lo:lo + CONV_CHUNK, :].astype(F32) * u_ref[lo:lo + CONV_CHUNK, :].astype(F32))
    w0 = cw_ref[0:1, :]
    w1 = cw_ref[1:2, :]
    w2 = cw_ref[2:3, :]
    for ch in range(SEQ // CONV_CHUNK):
        lo = ch * CONV_CHUNK
        base = CONV_PAD + lo
        conv = (w0 * z_ref[base - 1:base - 1 + CONV_CHUNK, :]
                + w1 * z_ref[base:base + CONV_CHUNK, :]
                + w2 * z_ref[base + 1:base + 1 + CONV_CHUNK, :])
        y = b_ref[lo:lo + CONV_CHUNK, :].astype(F32) * conv
        ms = jnp.mean(y * y, axis=-1, keepdims=True)
        o_ref[lo:lo + CONV_CHUNK, :] = (y * lax.rsqrt(ms + NORM_EPS) * nw_ref[...]).astype(BF16)


def _conv(proj, conv_w, conv_out_norm_w):
    col0 = 3 * NA_HEADS
    return pl.pallas_call(
        _conv_body,
        grid=(CONV_GROUPS,),
        in_specs=[
            pl.BlockSpec((SEQ, HEAD_DIM), lambda g: (0, col0 + g)),
            pl.BlockSpec((SEQ, HEAD_DIM), lambda g: (0, col0 + CONV_GROUPS + g)),
            pl.BlockSpec((SEQ, HEAD_DIM), lambda g: (0, col0 + 2 * CONV_GROUPS + g)),
            pl.BlockSpec((3, HEAD_DIM), lambda g: (0, g)),
            pl.BlockSpec((1, HEAD_DIM), lambda g: (0, g)),
        ],
        out_specs=pl.BlockSpec((SEQ, HEAD_DIM), lambda g: (0, g)),
        out_shape=jax.ShapeDtypeStruct((SEQ, CONV_WIDTH), BF16),
        scratch_shapes=[pltpu.VMEM((SEQ + 2 * CONV_PAD, HEAD_DIM), F32)],
        compiler_params=_cparams(("arbitrary",), 48),
        name="conv",
    )(proj, proj, proj, conv_w, conv_out_norm_w)


def _bf16_bits(x):
    return lax.bitcast_convert_type(x.astype(BF16).astype(F32), U32) & jnp.uint32(0xFFFF0000)


def _outproj_body(a_ref, c_ref, w_ref, x_ref, fw_ref, rwt_ref, rb_ref,
                  h_ref, hp_ref, ids_ref, gates_ref, ranks_ref, cnt_ref,
                  hacc_ref, carry_ref, tri_ref):
    m = pl.program_id(0)
    n = pl.program_id(1)

    @pl.when((m == 0) & (n == 0))
    def _():
        carry_ref[...] = jnp.zeros_like(carry_ref)
        r = lax.broadcasted_iota(I32, (OUT_TM, OUT_TM), 0)
        c = lax.broadcasted_iota(I32, (OUT_TM, OUT_TM), 1)
        tri_ref[...] = (r < c).astype(BF16)

    w = w_ref[...].astype(BF16)
    acc = (jnp.dot(a_ref[...], w[:ATTN_WIDTH], preferred_element_type=F32)
           + jnp.dot(c_ref[...], w[ATTN_WIDTH:], preferred_element_type=F32)
           + x_ref[...])
    h_ref[...] = acc
    hacc_ref[n] = acc

    @pl.when(n == OUT_NCH - 1)
    def _():
        ss = jnp.zeros((OUT_TM, 1), F32)
        for j in range(OUT_NCH):
            hj = hacc_ref[j]
            ss = ss + jnp.sum(hj * hj, axis=-1, keepdims=True)
        inv = lax.rsqrt(ss * (1.0 / D_MODEL) + NORM_EPS)
        hn = []
        logits = jnp.zeros((N_EXPERTS, OUT_TM), F32)
        for j in range(OUT_NCH):
            hj = (hacc_ref[j] * inv * fw_ref[:, j * OUT_TN:(j + 1) * OUT_TN]).astype(BF16)
            hn.append(hj)
            logits = logits + lax.dot_general(
                rwt_ref[:, j * OUT_TN:(j + 1) * OUT_TN], hj,
                (((1,), (1,)), ((), ())), preferred_element_type=F32)
        logits = logits + rb_ref[:, 0:1]
        half_chunks = OUT_NCH // 2
        for j in range(half_chunks):
            lo = _bf16_bits(hn[j].astype(F32)) >> 16
            hi = _bf16_bits(hn[j + half_chunks].astype(F32))
            hp_ref[:, j * OUT_TN:(j + 1) * OUT_TN] = lo | hi

        eio = lax.broadcasted_iota(I32, (N_EXPERTS, OUT_TM), 0)
        work = logits
        tops, idxs, hots = [], [], []
        for _ in range(TOP_K):
            mx = jnp.max(work, axis=0, keepdims=True)
            idx = jnp.min(jnp.where(work == mx, eio, N_EXPERTS), axis=0, keepdims=True)
            hot = eio == idx
            tops.append(mx)
            idxs.append(idx)
            hots.append(hot)
            work = jnp.where(hot, -jnp.inf, work)
        exps = [jnp.exp(t - tops[0]) for t in tops]
        den = exps[0] + exps[1] + exps[2] + exps[3]
        multi = jnp.zeros((N_EXPERTS, OUT_TM), F32)
        for hot in hots:
            multi = multi + hot.astype(F32)
        carry = carry_ref[:, 0:1]
        before = carry + jnp.dot(multi.astype(BF16), tri_ref[...], preferred_element_type=F32)
        zero_i = jnp.zeros((1, OUT_TM), I32)
        zero_f = jnp.zeros((1, OUT_TM), F32)
        for k in range(TOP_K):
            ids_ref[k:k + 1, :] = idxs[k]
            gates_ref[k:k + 1, :] = exps[k] / den
            rank = jnp.sum(jnp.where(hots[k], before, 0.0), axis=0, keepdims=True)
            ranks_ref[k:k + 1, :] = rank.astype(I32)
            ids_ref[TOP_K + k:TOP_K + k + 1, :] = zero_i
            gates_ref[TOP_K + k:TOP_K + k + 1, :] = zero_f
            ranks_ref[TOP_K + k:TOP_K + k + 1, :] = zero_i
        new_carry = carry + jnp.sum(multi, axis=1, keepdims=True)
        full = jnp.broadcast_to(new_carry, (N_EXPERTS, V7X_LANES))
        carry_ref[...] = full
        cnt_ref[...] = full


def _outproj_router(attn_n, conv_n, w_out, x2d, ffn_norm_w, router_wt, router_b_col):
    grid = (SEQ // OUT_TM, OUT_NCH)
    meta = lambda m, n: (0, m)
    return pl.pallas_call(
        _outproj_body,
        grid=grid,
        in_specs=[
            pl.BlockSpec((OUT_TM, ATTN_WIDTH), lambda m, n: (m, 0)),
            pl.BlockSpec((OUT_TM, CONV_WIDTH), lambda m, n: (m, 0)),
            pl.BlockSpec((D_MODEL, OUT_TN), lambda m, n: (0, n)),
            pl.BlockSpec((OUT_TM, OUT_TN), lambda m, n: (m, n)),
            pl.BlockSpec((1, D_MODEL), lambda m, n: (0, 0)),
            pl.BlockSpec((N_EXPERTS, D_MODEL), lambda m, n: (0, 0)),
            pl.BlockSpec((N_EXPERTS, V7X_LANES), lambda m, n: (0, 0)),
        ],
        out_specs=[
            pl.BlockSpec((OUT_TM, OUT_TN), lambda m, n: (m, n)),
            pl.BlockSpec((OUT_TM, HALF), lambda m, n: (m, 0)),
            pl.BlockSpec((2 * TOP_K, OUT_TM), meta),
            pl.BlockSpec((2 * TOP_K, OUT_TM), meta),
            pl.BlockSpec((2 * TOP_K, OUT_TM), meta),
            pl.BlockSpec((N_EXPERTS, V7X_LANES), lambda m, n: (0, 0)),
        ],
        out_shape=[
            jax.ShapeDtypeStruct((SEQ, D_MODEL), F32),
            jax.ShapeDtypeStruct((SEQ, HALF), U32),
            jax.ShapeDtypeStruct((2 * TOP_K, SEQ), I32),
            jax.ShapeDtypeStruct((2 * TOP_K, SEQ), F32),
            jax.ShapeDtypeStruct((2 * TOP_K, SEQ), I32),
            jax.ShapeDtypeStruct((N_EXPERTS, V7X_LANES), F32),
        ],
        scratch_shapes=[
            pltpu.VMEM((OUT_NCH, OUT_TM, OUT_TN), F32),
            pltpu.VMEM((N_EXPERTS, V7X_LANES), F32),
            pltpu.VMEM((OUT_TM, OUT_TM), BF16),
        ],
        compiler_params=_cparams(("arbitrary", "arbitrary"), 56),
        name="outproj_router",
    )(attn_n, conv_n, w_out, x2d, ffn_norm_w, router_wt, router_b_col)


HBM_ROW_TILE = 8
PAD_BITS = (128, 64, 32, 16, 8)


def _dispatch_body(dest_ref, pad_start_ref, pad_len_ref, hp_ref, xs_ref, zbuf_ref, sem, zsem):
    i = pl.program_id(0)

    def row_copy(t, d):
        return pltpu.make_async_copy(hp_ref.at[pl.ds(t, 1)], xs_ref.at[pl.ds(d, 1)], sem)

    def pad_head(e):
        return (-pad_start_ref[e]) & (HBM_ROW_TILE - 1)

    def pad_row_copy(e, j):
        return pltpu.make_async_copy(zbuf_ref.at[pl.ds(0, 1)],
                                     xs_ref.at[pl.ds(pad_start_ref[e] + j, 1)], zsem)

    def pad_piece_copy(e, bit):
        body = pad_len_ref[e] - pad_head(e)
        start = pad_start_ref[e] + pad_head(e) + (body & ~(2 * bit - 1))
        start = pl.multiple_of(start, HBM_ROW_TILE)
        return pltpu.make_async_copy(zbuf_ref.at[pl.ds(0, bit)], xs_ref.at[pl.ds(start, bit)], zsem)

    def for_each_pad(action):
        def per_expert(e, carry):
            for j in range(HBM_ROW_TILE - 1):
                @pl.when(j < pad_head(e))
                def _():
                    action(pad_row_copy(e, j))
            for bit in PAD_BITS:
                @pl.when(((pad_len_ref[e] - pad_head(e)) & bit) != 0)
                def _():
                    action(pad_piece_copy(e, bit))
            return carry
        lax.fori_loop(0, N_EXPERTS, per_expert, 0)

    @pl.when(i == 0)
    def _():
        zbuf_ref[...] = jnp.zeros_like(zbuf_ref)
        for_each_pad(lambda cp: cp.start())
        for_each_pad(lambda cp: cp.wait())

    def start_rows(t, carry):
        tok = i * DISPATCH_TB + t
        for k in range(TOP_K):
            row_copy(t, dest_ref[k * SEQ + tok]).start()
        return carry
    lax.fori_loop(0, DISPATCH_TB, start_rows, 0, unroll=4)

    def wait_rows(t, carry):
        tok = i * DISPATCH_TB + t
        for k in range(TOP_K):
            row_copy(t, dest_ref[k * SEQ + tok]).wait()
        return carry
    lax.fori_loop(0, DISPATCH_TB, wait_rows, 0, unroll=4)


def _dispatch(dest_flat, pad_start, pad_len, hp):
    grid_spec = pltpu.PrefetchScalarGridSpec(
        num_scalar_prefetch=3,
        grid=(SEQ // DISPATCH_TB,),
        in_specs=[pl.BlockSpec((DISPATCH_TB, HALF), lambda i, d, ps, pn: (i, 0))],
        out_specs=pl.BlockSpec(memory_space=pl.ANY),
        scratch_shapes=[
            pltpu.VMEM((PAD_BITS[0], HALF), U32),
            pltpu.SemaphoreType.DMA(()),
            pltpu.SemaphoreType.DMA(()),
        ],
    )
    return pl.pallas_call(
        _dispatch_body,
        grid_spec=grid_spec,
        out_shape=jax.ShapeDtypeStruct((N_SLOTS, HALF), U32),
        compiler_params=pltpu.CompilerParams(dimension_semantics=("arbitrary",),
                                             has_side_effects=True),
        name="dispatch",
    )(dest_flat, pad_start, pad_len, hp)


def _unpack_rows(words):
    lo = lax.bitcast_convert_type(words << 16, F32).astype(BF16)
    hi = lax.bitcast_convert_type(words & jnp.uint32(0xFFFF0000), F32).astype(BF16)
    return lo, hi


def _cast_weight(src_ref, dst_ref, slot):
    def chunk(i, carry):
        rows = pl.ds(pl.multiple_of(i * MOE_SUB, MOE_SUB), MOE_SUB)
        dst_ref[slot, rows, :] = src_ref[0, rows, :].astype(BF16)
        return carry
    lax.fori_loop(0, D_MODEL // MOE_SUB, chunk, 0)


def _moe_body(be_ref, bx_ref, ns_ref, nu_ref,
              x_ref, wg_ref, wl_ref, wd_ref, bg_ref, bl_ref, bd_ref,
              o_ref, wbf_ref, act_ref):
    b = pl.program_id(0)
    c = pl.program_id(1)
    valid = b < nu_ref[0]
    nsub = ns_ref[b]

    @pl.when(valid & (c < MOE_NCH))
    def _():
        _cast_weight(wg_ref, wbf_ref, 0)
        _cast_weight(wl_ref, wbf_ref, 1)
        bg = bg_ref[0]
        bl = bl_ref[0]

        def sub(i, carry):
            rows = pl.ds(pl.multiple_of(i * MOE_SUB, MOE_SUB), MOE_SUB)
            lo, hi = _unpack_rows(x_ref[rows, :])
            g = (jnp.dot(lo, wbf_ref[0, :HALF, :], preferred_element_type=F32)
                 + jnp.dot(hi, wbf_ref[0, HALF:, :], preferred_element_type=F32) + bg)
            lin = (jnp.dot(lo, wbf_ref[1, :HALF, :], preferred_element_type=F32)
                   + jnp.dot(hi, wbf_ref[1, HALF:, :], preferred_element_type=F32) + bl)
            g = jnp.minimum(g, SWIGLU_LIMIT)
            lin = jnp.clip(lin, -SWIGLU_LIMIT, SWIGLU_LIMIT)
            act = g * jax.nn.sigmoid(SWIGLU_ALPHA * g) * (lin + 1.0)
            act_ref[c, rows, :] = act.astype(BF16)
            return carry
        lax.fori_loop(0, nsub, sub, 0)

    @pl.when(valid & (c >= MOE_NCH))
    def _():
        _cast_weight(wd_ref, wbf_ref, 0)
        bd = bd_ref[0]

        def sub(i, carry):
            rows = pl.ds(pl.multiple_of(i * MOE_SUB, MOE_SUB), MOE_SUB)
            y = jnp.zeros((MOE_SUB, MOE_CH), F32) + bd
            for k in range(MOE_NCH):
                y = y + jnp.dot(act_ref[k, rows, :], wbf_ref[0, k * MOE_CH:(k + 1) * MOE_CH, :],
                                preferred_element_type=F32)
            o_ref[rows, :] = y
            return carry
        lax.fori_loop(0, nsub, sub, 0)

        def fill(i, carry):
            rows = pl.ds(pl.multiple_of(i * MOE_SUB, MOE_SUB), MOE_SUB)
            o_ref[rows, :] = jnp.zeros((MOE_SUB, MOE_CH), F32)
            return carry
        lax.fori_loop(nsub, MOE_TM // MOE_SUB, fill, 0)


def _moe(blk_expert, blk_x, blk_nsub, n_used, xs, w_gate_up, w_down, b_gate_up, b_down):
    last = MOE_NCH - 1

    def up_chunk(b, c, nu):
        return jnp.where(b < nu[0], jnp.minimum(c, last), last)

    def down_chunk(b, c, nu):
        return jnp.where(b < nu[0], jnp.maximum(c - MOE_NCH, 0), last)

    grid_spec = pltpu.PrefetchScalarGridSpec(
        num_scalar_prefetch=4,
        grid=(MOE_NB, 2 * MOE_NCH),
        in_specs=[
            pl.BlockSpec((MOE_TM, HALF), lambda b, c, be, bx, ns, nu: (bx[b], 0)),
            pl.BlockSpec((1, D_MODEL, MOE_CH),
                         lambda b, c, be, bx, ns, nu: (be[b], 0, up_chunk(b, c, nu))),
            pl.BlockSpec((1, D_MODEL, MOE_CH),
                         lambda b, c, be, bx, ns, nu: (be[b], 0, MOE_NCH + up_chunk(b, c, nu))),
            pl.BlockSpec((1, D_FF, MOE_CH),
                         lambda b, c, be, bx, ns, nu: (be[b], 0, down_chunk(b, c, nu))),
            pl.BlockSpec((1, 1, MOE_CH),
                         lambda b, c, be, bx, ns, nu: (be[b], 0, up_chunk(b, c, nu))),
            pl.BlockSpec((1, 1, MOE_CH),
                         lambda b, c, be, bx, ns, nu: (be[b], 0, MOE_NCH + up_chunk(b, c, nu))),
            pl.BlockSpec((1, 1, MOE_CH),
                         lambda b, c, be, bx, ns, nu: (be[b], 0, down_chunk(b, c, nu))),
        ],
        out_specs=pl.BlockSpec((MOE_TM, MOE_CH),
                               lambda b, c, be, bx, ns, nu: (bx[b], down_chunk(b, c, nu))),
        scratch_shapes=[
            pltpu.VMEM((2, D_MODEL, MOE_CH), BF16),
            pltpu.VMEM((MOE_NCH, MOE_TM, MOE_CH), BF16),
        ],
    )
    return pl.pallas_call(
        _moe_body,
        grid_spec=grid_spec,
        out_shape=jax.ShapeDtypeStruct((N_SLOTS, D_MODEL), F32),
        compiler_params=_cparams(("arbitrary", "arbitrary"), 60),
        name="moe",
    )(blk_expert, blk_x, blk_nsub, n_used, xs, w_gate_up, w_gate_up, w_down,
      b_gate_up, b_gate_up, b_down)


def _combine_body(dest_ref, gates_ref, h_ref, nw_ref, ys_ref, o_ref, buf_ref, sem):
    i = pl.program_id(0)
    nblk = pl.num_programs(0)

    def row_copy(blk, slot, k, t):
        d = dest_ref[k * SEQ + blk * COMBINE_TB + t]
        return pltpu.make_async_copy(ys_ref.at[pl.ds(d, 1)],
                                     buf_ref.at[slot, k, pl.ds(t, 1)], sem.at[slot])

    def start_block(blk, slot):
        def body(t, carry):
            for k in range(TOP_K):
                row_copy(blk, slot, k, t).start()
            return carry
        lax.fori_loop(0, COMBINE_TB, body, 0)

    def wait_block(blk, slot):
        def body(t, carry):
            for k in range(TOP_K):
                row_copy(blk, slot, k, t).wait()
            return carry
        lax.fori_loop(0, COMBINE_TB, body, 0)

    slot = i % 2

    @pl.when(i == 0)
    def _():
        start_block(0, 0)

    @pl.when(i + 1 < nblk)
    def _():
        start_block(i + 1, 1 - slot)

    wait_block(i, slot)

    acc = h_ref[...]
    for k in range(TOP_K):
        acc = acc + gates_ref[:, k:k + 1] * buf_ref[slot, k]
    ms = jnp.mean(acc * acc, axis=-1, keepdims=True)
    o_ref[...] = acc * lax.rsqrt(ms + NORM_EPS) * nw_ref[...]


def _combine(dest_flat, gates_t, h, final_norm_w, ys):
    grid_spec = pltpu.PrefetchScalarGridSpec(
        num_scalar_prefetch=1,
        grid=(SEQ // COMBINE_TB,),
        in_specs=[
            pl.BlockSpec((COMBINE_TB, TOP_K), lambda i, d: (i, 0)),
            pl.BlockSpec((COMBINE_TB, D_MODEL), lambda i, d: (i, 0)),
            pl.BlockSpec((1, D_MODEL), lambda i, d: (0, 0)),
            pl.BlockSpec(memory_space=pl.ANY),
        ],
        out_specs=pl.BlockSpec((COMBINE_TB, D_MODEL), lambda i, d: (i, 0)),
        scratch_shapes=[
            pltpu.VMEM((2, TOP_K, COMBINE_TB, D_MODEL), F32),
            pltpu.SemaphoreType.DMA((2,)),
        ],
    )
    return pl.pallas_call(
        _combine_body,
        grid_spec=grid_spec,
        out_shape=jax.ShapeDtypeStruct((SEQ, D_MODEL), F32),
        compiler_params=_cparams(("arbitrary",), 40),
        name="combine",
    )(dest_flat, gates_t, h, final_norm_w, ys)


def _routing_tables(counts, ids, ranks):
    nblk = (counts + MOE_TM - 1) // MOE_TM
    cum = jnp.cumsum(nblk)
    first_blk = cum - nblk
    n_used = cum[-1]
    first_of = jnp.zeros(ids.shape, I32)
    for e in range(N_EXPERTS):
        first_of = first_of + jnp.where(ids == e, first_blk[e], 0)
    dest = first_of * MOE_TM + ranks
    b = jnp.arange(MOE_NB, dtype=I32)
    valid = b < n_used
    blk_e = jnp.minimum(jnp.searchsorted(cum, b, side="right"), N_EXPERTS - 1).astype(I32)
    blk_e = jnp.where(valid, blk_e, blk_e[n_used - 1])
    rows = jnp.clip(counts[blk_e] - (b - first_blk[blk_e]) * MOE_TM, 0, MOE_TM)
    blk_nsub = jnp.where(valid, (rows + MOE_SUB - 1) // MOE_SUB, 0).astype(I32)
    blk_x = jnp.where(valid, b, n_used - 1).astype(I32)
    pad_start = (first_blk * MOE_TM + counts).astype(I32)
    pad_len = ((-counts) % MOE_SUB).astype(I32)
    return (dest.reshape(-1).astype(I32), blk_e, blk_x, blk_nsub,
            n_used.reshape(1).astype(I32), pad_start, pad_len)


def kernel(x, attn_norm_w, w_in, conv_w, rel_pos_bias, attn_out_norm_w, conv_out_norm_w, w_out,
           ffn_norm_w, router_w, router_b, w_gate_up, b_gate_up, w_down, b_down, final_norm_w):
    bsz, s, d = x.shape
    assert (bsz, s, d) == (1, SEQ, D_MODEL)
    assert attn_norm_w.shape[0] == 1
    x2d = x.reshape(SEQ, D_MODEL)

    proj = _inproj(x2d, attn_norm_w[0].reshape(1, D_MODEL), w_in[0])
    attn_n = _attention(proj, _bias_table(rel_pos_bias[0]), attn_out_norm_w[0].reshape(1, ATTN_WIDTH))
    conv_n = _conv(proj, conv_w[0], conv_out_norm_w[0].reshape(1, CONV_WIDTH))

    router_wt = router_w[0].T.astype(BF16)
    router_b_col = jnp.broadcast_to(router_b[0][:, None], (N_EXPERTS, V7X_LANES))
    h, hp, ids, gates, ranks, cnt = _outproj_router(
        attn_n, conv_n, w_out[0], x2d, ffn_norm_w[0].reshape(1, D_MODEL), router_wt, router_b_col)

    counts = cnt[:, 0].astype(I32)
    dest, blk_e, blk_x, blk_nsub, n_used, pad_start, pad_len = _routing_tables(
        counts, ids[:TOP_K], ranks[:TOP_K])

    xs = _dispatch(dest, pad_start, pad_len, hp)
    ys = _moe(blk_e, blk_x, blk_nsub, n_used, xs, w_gate_up[0], w_down[0],
              b_gate_up[0].reshape(N_EXPERTS, 1, 2 * D_FF), b_down[0].reshape(N_EXPERTS, 1, D_MODEL))
    out = _combine(dest, gates[:TOP_K].T, h, final_norm_w.reshape(1, D_MODEL), ys)
    return out.reshape(bsz, s, d)
```

```python
import functools
import math

import jax
import jax.numpy as jnp
from jax import lax
from jax.experimental import pallas as pl
from jax.experimental.pallas import tpu as pltpu

F32 = jnp.float32
BF16 = jnp.bfloat16
I32 = jnp.int32
U32 = jnp.uint32

D_MODEL = 2048
SEQ = 8192
HEAD_DIM = 128
NA_HEADS = 8
ATTN_WIDTH = NA_HEADS * HEAD_DIM
CONV_WIDTH = D_MODEL - ATTN_WIDTH
CONV_GROUPS = CONV_WIDTH // HEAD_DIM
N_PROJ = 3 * ATTN_WIDTH + 3 * CONV_WIDTH
GRID_W = 64
GRID_ROWS = SEQ // GRID_W
NA_KH = 8
NA_KW = 16
N_EXPERTS = 32
TOP_K = 4
D_FF = D_MODEL
SWIGLU_LIMIT = 7.0
SWIGLU_ALPHA = 1.702
NORM_EPS = 1e-5
NEG_INF = -1e30

V7X_LANES = 128
V7X_VMEM_BYTES = 64 * 1024 * 1024
MIB = 1024 * 1024

INPROJ_TM = 1024
INPROJ_TN = 512
NORM_ROWS = 256
ATTN_ROWS_PER_STEP = 8
WIN_KEYS = NA_KH * GRID_W
CONV_CHUNK = 1024
OUT_TM = 1024
OUT_TN = 512
OUT_NCH = D_MODEL // OUT_TN
MOE_SUB = 256
MOE_TM = 5 * MOE_SUB
MOE_CH = 512
MOE_NCH = D_FF // MOE_CH
MOE_NB = (SEQ * TOP_K) // MOE_TM + 1 + N_EXPERTS - 1
N_SLOTS = MOE_NB * MOE_TM
HALF = D_MODEL // 2
DISPATCH_TB = 1024
COMBINE_TB = 256


def _cparams(semantics, vmem_mib):
    return pltpu.CompilerParams(dimension_semantics=semantics,
                                vmem_limit_bytes=vmem_mib * MIB)


def _inproj_body(x_ref, nw_ref, w_ref, o_ref, xn_ref):
    @pl.when(pl.program_id(1) == 0)
    def _():
        def chunk(i, carry):
            rows = pl.ds(pl.multiple_of(i * NORM_ROWS, NORM_ROWS), NORM_ROWS)
            x = x_ref[rows, :]
            ms = jnp.mean(x * x, axis=-1, keepdims=True)
            xn_ref[rows, :] = (x * lax.rsqrt(ms + NORM_EPS) * nw_ref[...]).astype(BF16)
            return carry
        lax.fori_loop(0, INPROJ_TM // NORM_ROWS, chunk, 0)

    o_ref[...] = jnp.dot(xn_ref[...], w_ref[...].astype(BF16),
                         preferred_element_type=F32).astype(BF16)


def _inproj(x2d, norm_w, w_in):
    return pl.pallas_call(
        _inproj_body,
        grid=(SEQ // INPROJ_TM, N_PROJ // INPROJ_TN),
        in_specs=[
            pl.BlockSpec((INPROJ_TM, D_MODEL), lambda m, n: (m, 0)),
            pl.BlockSpec((1, D_MODEL), lambda m, n: (0, 0)),
            pl.BlockSpec((D_MODEL, INPROJ_TN), lambda m, n: (0, n)),
        ],
        out_specs=pl.BlockSpec((INPROJ_TM, INPROJ_TN), lambda m, n: (m, n)),
        out_shape=jax.ShapeDtypeStruct((SEQ, N_PROJ), BF16),
        scratch_shapes=[pltpu.VMEM((INPROJ_TM, D_MODEL), BF16)],
        compiler_params=_cparams(("parallel", "arbitrary"), 48),
        name="inproj",
    )(x2d, norm_w, w_in)


def _bias_table(rel_pos_bias):
    cq = jnp.arange(GRID_W)[:, None]
    ck = jnp.arange(GRID_W)[None, :]
    cs = jnp.clip(cq - NA_KW // 2, 0, GRID_W - NA_KW)
    band = (ck >= cs) & (ck < cs + NA_KW)
    dc = jnp.clip(ck - cq, -(NA_KW - 1), NA_KW - 1) + (NA_KW - 1)
    t = jnp.zeros((NA_HEADS, 2 * NA_KH - 1, GRID_W, GRID_W), F32)
    for j in range(2 * NA_KW - 1):
        t = t + jnp.where(dc == j, rel_pos_bias[:, :, j, None, None], 0.0)
    t = jnp.where(band[None, None], t, NEG_INF)
    tb = jnp.stack([t[:, off:off + NA_KH] for off in range(NA_KH)], axis=1)
    return tb.transpose(0, 1, 3, 2, 4).reshape(NA_HEADS, NA_KH, GRID_W, WIN_KEYS).astype(F32)


def _attn_body(q_ref, k_ref, v_ref, tb_ref, nw_ref, o_ref):
    rb = pl.program_id(1)
    scale = 1.0 / math.sqrt(HEAD_DIM)
    for i in range(ATTN_ROWS_PER_STEP):
        r = rb * ATTN_ROWS_PER_STEP + i
        ws = jnp.clip(r - NA_KH // 2, 0, GRID_ROWS - NA_KH)
        off = ws - r + (NA_KH - 1)
        kstart = pl.multiple_of(ws * GRID_W, GRID_W)
        kk = k_ref[pl.ds(kstart, WIN_KEYS), :]
        vv = v_ref[pl.ds(kstart, WIN_KEYS), :]
        q = q_ref[i * GRID_W:(i + 1) * GRID_W, :]
        s = lax.dot_general(q, kk, (((1,), (1,)), ((), ())), preferred_element_type=F32)
        s = s * scale + tb_ref[0, off]
        m = jnp.max(s, axis=-1, keepdims=True)
        p = jnp.exp(s - m)
        l = jnp.sum(p, axis=-1, keepdims=True)
        o = jnp.dot(p.astype(BF16), vv, preferred_element_type=F32) / l
        ms = jnp.mean(o * o, axis=-1, keepdims=True)
        o_ref[i * GRID_W:(i + 1) * GRID_W, :] = (
            o * lax.rsqrt(ms + NORM_EPS) * nw_ref[...]).astype(BF16)


def _attention(proj, bias_tab, attn_out_norm_w):
    rows = ATTN_ROWS_PER_STEP * GRID_W
    return pl.pallas_call(
        _attn_body,
        grid=(NA_HEADS, GRID_ROWS // ATTN_ROWS_PER_STEP),
        in_specs=[
            pl.BlockSpec((rows, HEAD_DIM), lambda h, rb: (rb, h)),
            pl.BlockSpec((SEQ, HEAD_DIM), lambda h, rb: (0, NA_HEADS + h)),
            pl.BlockSpec((SEQ, HEAD_DIM), lambda h, rb: (0, 2 * NA_HEADS + h)),
            pl.BlockSpec((1, NA_KH, GRID_W, WIN_KEYS), lambda h, rb: (h, 0, 0, 0)),
            pl.BlockSpec((1, HEAD_DIM), lambda h, rb: (0, h)),
        ],
        out_specs=pl.BlockSpec((rows, HEAD_DIM), lambda h, rb: (rb, h)),
        out_shape=jax.ShapeDtypeStruct((SEQ, ATTN_WIDTH), BF16),
        compiler_params=_cparams(("parallel", "arbitrary"), 32),
        name="attn",
    )(proj, proj, proj, bias_tab, attn_out_norm_w)


CONV_PAD = 8


def _conv_body(u_ref, b_ref, c_ref, cw_ref, nw_ref, o_ref, z_ref):
    zeros = jnp.zeros((CONV_PAD, HEAD_DIM), F32)
    z_ref[0:CONV_PAD, :] = zeros
    z_ref[CONV_PAD + SEQ:2 * CONV_PAD + SEQ, :] = zeros
    for ch in range(SEQ // CONV_CHUNK):
        lo = ch * CONV_CHUNK
        z_ref[CONV_PAD + lo:CONV_PAD + lo + CONV_CHUNK, :] = (
            c_ref[lo:lo + CONV_CHUNK, :].astype(F32) * u_ref[lo:lo + CONV_CHUNK, :].astype(F32))
    w0 = cw_ref[0:1, :]
    w1 = cw_ref[1:2, :]
    w2 = cw_ref[2:3, :]
    for ch in range(SEQ // CONV_CHUNK):
        lo = ch * CONV_CHUNK
        base = CONV_PAD + lo
        conv = (w0 * z_ref[base - 1:base - 1 + CONV_CHUNK, :]
                + w1 * z_ref[base:base + CONV_CHUNK, :]
                + w2 * z_ref[base + 1:base + 1 + CONV_CHUNK, :])
        y = b_ref[lo:lo + CONV_CHUNK, :].astype(F32) * conv
        ms = jnp.mean(y * y, axis=-1, keepdims=True)
        o_ref[lo:lo + CONV_CHUNK, :] = (y * lax.rsqrt(ms + NORM_EPS) * nw_ref[...]).astype(BF16)


def _conv(proj, conv_w, conv_out_norm_w):
    col0 = 3 * NA_HEADS
    return pl.pallas_call(
        _conv_body,
        grid=(CONV_GROUPS,),
        in_specs=[
            pl.BlockSpec((SEQ, HEAD_DIM), lambda g: (0, col0 + g)),
            pl.BlockSpec((SEQ, HEAD_DIM), lambda g: (0, col0 + CONV_GROUPS + g)),
            pl.BlockSpec((SEQ, HEAD_DIM), lambda g: (0, col0 + 2 * CONV_GROUPS + g)),
            pl.BlockSpec((3, HEAD_DIM), lambda g: (0, g)),
            pl.BlockSpec((1, HEAD_DIM), lambda g: (0, g)),
        ],
        out_specs=pl.BlockSpec((SEQ, HEAD_DIM), lambda g: (0, g)),
        out_shape=jax.ShapeDtypeStruct((SEQ, CONV_WIDTH), BF16),
        scratch_shapes=[pltpu.VMEM((SEQ + 2 * CONV_PAD, HEAD_DIM), F32)],
        compiler_params=_cparams(("arbitrary",), 48),
        name="conv",
    )(proj, proj, proj, conv_w, conv_out_norm_w)


def _bf16_bits(x):
    return lax.bitcast_convert_type(x.astype(BF16).astype(F32), U32) & jnp.uint32(0xFFFF0000)


def _outproj_body(a_ref, c_ref, w_ref, x_ref, fw_ref, rwt_ref, rb_ref,
                  h_ref, hp_ref, ids_ref, gates_ref, ranks_ref, cnt_ref,
                  hacc_ref, carry_ref, tri_ref):
    m = pl.program_id(0)
    n = pl.program_id(1)

    @pl.when((m == 0) & (n == 0))
    def _():
        carry_ref[...] = jnp.zeros_like(carry_ref)
        r = lax.broadcasted_iota(I32, (OUT_TM, OUT_TM), 0)
        c = lax.broadcasted_iota(I32, (OUT_TM, OUT_TM), 1)
        tri_ref[...] = (r < c).astype(BF16)

    w = w_ref[...].astype(BF16)
    acc = (jnp.dot(a_ref[...], w[:ATTN_WIDTH], preferred_element_type=F32)
           + jnp.dot(c_ref[...], w[ATTN_WIDTH:], preferred_element_type=F32)
           + x_ref[...])
    h_ref[...] = acc
    hacc_ref[n] = acc

    @pl.when(n == OUT_NCH - 1)
    def _():
        ss = jnp.zeros((OUT_TM, 1), F32)
        for j in range(OUT_NCH):
            hj = hacc_ref[j]
            ss = ss + jnp.sum(hj * hj, axis=-1, keepdims=True)
        inv = lax.rsqrt(ss * (1.0 / D_MODEL) + NORM_EPS)
        hn = []
        logits = jnp.zeros((N_EXPERTS, OUT_TM), F32)
        for j in range(OUT_NCH):
            hj = (hacc_ref[j] * inv * fw_ref[:, j * OUT_TN:(j + 1) * OUT_TN]).astype(BF16)
            hn.append(hj)
            logits = logits + lax.dot_general(
                rwt_ref[:, j * OUT_TN:(j + 1) * OUT_TN], hj,
                (((1,), (1,)), ((), ())), preferred_element_type=F32)
        logits = logits + rb_ref[:, 0:1]
        half_chunks = OUT_NCH // 2
        for j in range(half_chunks):
            lo = _bf16_bits(hn[j].astype(F32)) >> 16
            hi = _bf16_bits(hn[j + half_chunks].astype(F32))
            hp_ref[:, j * OUT_TN:(j + 1) * OUT_TN] = lo | hi

        eio = lax.broadcasted_iota(I32, (N_EXPERTS, OUT_TM), 0)
        work = logits
        tops, idxs, hots = [], [], []
        for _ in range(TOP_K):
            mx = jnp.max(work, axis=0, keepdims=True)
            idx = jnp.min(jnp.where(work == mx, eio, N_EXPERTS), axis=0, keepdims=True)
            hot = eio == idx
            tops.append(mx)
            idxs.append(idx)
            hots.append(hot)
            work = jnp.where(hot, -jnp.inf, work)
        exps = [jnp.exp(t - tops[0]) for t in tops]
        den = exps[0] + exps[1] + exps[2] + exps[3]
        multi = jnp.zeros((N_EXPERTS, OUT_TM), F32)
        for hot in hots:
            multi = multi + hot.astype(F32)
        carry = carry_ref[:, 0:1]
        before = carry + jnp.dot(multi.astype(BF16), tri_ref[...], preferred_element_type=F32)
        zero_i = jnp.zeros((1, OUT_TM), I32)
        zero_f = jnp.zeros((1, OUT_TM), F32)
        for k in range(TOP_K):
            ids_ref[k:k + 1, :] = idxs[k]
            gates_ref[k:k + 1, :] = exps[k] / den
            rank = jnp.sum(jnp.where(hots[k], before, 0.0), axis=0, keepdims=True)
            ranks_ref[k:k + 1, :] = rank.astype(I32)
            ids_ref[TOP_K + k:TOP_K + k + 1, :] = zero_i
            gates_ref[TOP_K + k:TOP_K + k + 1, :] = zero_f
            ranks_ref[TOP_K + k:TOP_K + k + 1, :] = zero_i
        new_carry = carry + jnp.sum(multi, axis=1, keepdims=True)
        full = jnp.broadcast_to(new_carry, (N_EXPERTS, V7X_LANES))
        carry_ref[...] = full
        cnt_ref[...] = full


def _outproj_router(attn_n, conv_n, w_out, x2d, ffn_norm_w, router_wt, router_b_col):
    grid = (SEQ // OUT_TM, OUT_NCH)
    meta = lambda m, n: (0, m)
    return pl.pallas_call(
        _outproj_body,
        grid=grid,
        in_specs=[
            pl.BlockSpec((OUT_TM, ATTN_WIDTH), lambda m, n: (m, 0)),
            pl.BlockSpec((OUT_TM, CONV_WIDTH), lambda m, n: (m, 0)),
            pl.BlockSpec((D_MODEL, OUT_TN), lambda m, n: (0, n)),
            pl.BlockSpec((OUT_TM, OUT_TN), lambda m, n: (m, n)),
            pl.BlockSpec((1, D_MODEL), lambda m, n: (0, 0)),
            pl.BlockSpec((N_EXPERTS, D_MODEL), lambda m, n: (0, 0)),
            pl.BlockSpec((N_EXPERTS, V7X_LANES), lambda m, n: (0, 0)),
        ],
        out_specs=[
            pl.BlockSpec((OUT_TM, OUT_TN), lambda m, n: (m, n)),
            pl.BlockSpec((OUT_TM, HALF), lambda m, n: (m, 0)),
            pl.BlockSpec((2 * TOP_K, OUT_TM), meta),
            pl.BlockSpec((2 * TOP_K, OUT_TM), meta),
            pl.BlockSpec((2 * TOP_K, OUT_TM), meta),
            pl.BlockSpec((N_EXPERTS, V7X_LANES), lambda m, n: (0, 0)),
        ],
        out_shape=[
            jax.ShapeDtypeStruct((SEQ, D_MODEL), F32),
            jax.ShapeDtypeStruct((SEQ, HALF), U32),
            jax.ShapeDtypeStruct((2 * TOP_K, SEQ), I32),
            jax.ShapeDtypeStruct((2 * TOP_K, SEQ), F32),
            jax.ShapeDtypeStruct((2 * TOP_K, SEQ), I32),
            jax.ShapeDtypeStruct((N_EXPERTS, V7X_LANES), F32),
        ],
        scratch_shapes=[
            pltpu.VMEM((OUT_NCH, OUT_TM, OUT_TN), F32),
            pltpu.VMEM((N_EXPERTS, V7X_LANES), F32),
            pltpu.VMEM((OUT_TM, OUT_TM), BF16),
        ],
        compiler_params=_cparams(("arbitrary", "arbitrary"), 56),
        name="outproj_router",
    )(attn_n, conv_n, w_out, x2d, ffn_norm_w, router_wt, router_b_col)


HBM_ROW_TILE = 8
PAD_BITS = (128, 64, 32, 16, 8)


def _dispatch_body(dest_ref, pad_start_ref, pad_len_ref, hp_ref, xs_ref, zbuf_ref, sem, zsem):
    i = pl.program_id(0)

    def row_copy(t, d):
        return pltpu.make_async_copy(hp_ref.at[pl.ds(t, 1)], xs_ref.at[pl.ds(d, 1)], sem)

    def pad_head(e):
        return (-pad_start_ref[e]) & (HBM_ROW_TILE - 1)

    def pad_row_copy(e, j):
        return pltpu.make_async_copy(zbuf_ref.at[pl.ds(0, 1)],
                                     xs_ref.at[pl.ds(pad_start_ref[e] + j, 1)], zsem)

    def pad_piece_copy(e, bit):
        body = pad_len_ref[e] - pad_head(e)
        start = pad_start_ref[e] + pad_head(e) + (body & ~(2 * bit - 1))
        start = pl.multiple_of(start, HBM_ROW_TILE)
        return pltpu.make_async_copy(zbuf_ref.at[pl.ds(0, bit)], xs_ref.at[pl.ds(start, bit)], zsem)

    def for_each_pad(action):
        def per_expert(e, carry):
            for j in range(HBM_ROW_TILE - 1):
                @pl.when(j < pad_head(e))
                def _():
                    action(pad_row_copy(e, j))
            for bit in PAD_BITS:
                @pl.when(((pad_len_ref[e] - pad_head(e)) & bit) != 0)
                def _():
                    action(pad_piece_copy(e, bit))
            return carry
        lax.fori_loop(0, N_EXPERTS, per_expert, 0)

    @pl.when(i == 0)
    def _():
        zbuf_ref[...] = jnp.zeros_like(zbuf_ref)
        for_each_pad(lambda cp: cp.start())
        for_each_pad(lambda cp: cp.wait())

    def start_rows(t, carry):
        tok = i * DISPATCH_TB + t
        for k in range(TOP_K):
            row_copy(t, dest_ref[k * SEQ + tok]).start()
        return carry
    lax.fori_loop(0, DISPATCH_TB, start_rows, 0, unroll=4)

    def wait_rows(t, carry):
        tok = i * DISPATCH_TB + t
        for k in range(TOP_K):
            row_copy(t, dest_ref[k * SEQ + tok]).wait()
        return carry
    lax.fori_loop(0, DISPATCH_TB, wait_rows, 0, unroll=4)


def _dispatch(dest_flat, pad_start, pad_len, hp):
    grid_spec = pltpu.PrefetchScalarGridSpec(
        num_scalar_prefetch=3,
        grid=(SEQ // DISPATCH_TB,),
        in_specs=[pl.BlockSpec((DISPATCH_TB, HALF), lambda i, d, ps, pn: (i, 0))],
        out_specs=pl.BlockSpec(memory_space=pl.ANY),
        scratch_shapes=[
            pltpu.VMEM((PAD_BITS[0], HALF), U32),
            pltpu.SemaphoreType.DMA(()),
            pltpu.SemaphoreType.DMA(()),
        ],
    )
    return pl.pallas_call(
        _dispatch_body,
        grid_spec=grid_spec,
        out_shape=jax.ShapeDtypeStruct((N_SLOTS, HALF), U32),
        compiler_params=pltpu.CompilerParams(dimension_semantics=("arbitrary",),
                                             has_side_effects=True),
        name="dispatch",
    )(dest_flat, pad_start, pad_len, hp)


def _unpack_rows(words):
    lo = lax.bitcast_convert_type(words << 16, F32).astype(BF16)
    hi = lax.bitcast_convert_type(words & jnp.uint32(0xFFFF0000), F32).astype(BF16)
    return lo, hi


MOE_PASS = 2 * MOE_SUB


def _for_each_pass(nsub, fn):
    npair = lax.shift_right_logical(nsub, 1)

    def pair(i, carry):
        fn(pl.multiple_of(i * MOE_PASS, MOE_PASS), MOE_PASS)
        return carry
    lax.fori_loop(0, npair, pair, 0)

    @pl.when((nsub & 1) == 1)
    def _():
        fn(pl.multiple_of(npair * MOE_PASS, MOE_SUB), MOE_SUB)


def _moe_body(be_ref, bx_ref, ns_ref, nu_ref,
              x_ref, wg_ref, wl_ref, wd_ref, bg_ref, bl_ref, bd_ref,
              o_ref, act_ref):
    b = pl.program_id(0)
    c = pl.program_id(1)
    valid = b < nu_ref[0]
    nsub = ns_ref[b]

    @pl.when(valid & (c < MOE_NCH))
    def _():
        def up(first, n_rows):
            rows = pl.ds(first, n_rows)
            lo, hi = _unpack_rows(x_ref[rows, :])
            g = (jnp.dot(lo, wg_ref[0, :HALF, :].astype(BF16), preferred_element_type=F32)
                 + jnp.dot(hi, wg_ref[0, HALF:, :].astype(BF16), preferred_element_type=F32)
                 + bg_ref[0])
            lin = (jnp.dot(lo, wl_ref[0, :HALF, :].astype(BF16), preferred_element_type=F32)
                   + jnp.dot(hi, wl_ref[0, HALF:, :].astype(BF16), preferred_element_type=F32)
                   + bl_ref[0])
            g = jnp.minimum(g, SWIGLU_LIMIT)
            lin = jnp.clip(lin, -SWIGLU_LIMIT, SWIGLU_LIMIT)
            act = g * jax.nn.sigmoid(SWIGLU_ALPHA * g) * (lin + 1.0)
            act_ref[c, rows, :] = act.astype(BF16)
        _for_each_pass(nsub, up)

    @pl.when(valid & (c >= MOE_NCH))
    def _():
        def down(first, n_rows):
            rows = pl.ds(first, n_rows)
            y = jnp.zeros((n_rows, MOE_CH), F32) + bd_ref[0]
            for k in range(MOE_NCH):
                y = y + jnp.dot(act_ref[k, rows, :],
                                wd_ref[0, k * MOE_CH:(k + 1) * MOE_CH, :].astype(BF16),
                                preferred_element_type=F32)
            o_ref[rows, :] = y
        _for_each_pass(nsub, down)

        def fill(i, carry):
            rows = pl.ds(pl.multiple_of(i * MOE_SUB, MOE_SUB), MOE_SUB)
            o_ref[rows, :] = jnp.zeros((MOE_SUB, MOE_CH), F32)
            return carry
        lax.fori_loop(nsub, MOE_TM // MOE_SUB, fill, 0)


def _moe(blk_expert, blk_x, blk_nsub, n_used, xs, w_gate_up, w_down, b_gate_up, b_down):
    last = MOE_NCH - 1

    def up_chunk(b, c, nu):
        return jnp.where(b < nu[0], jnp.minimum(c, last), last)

    def down_chunk(b, c, nu):
        return jnp.where(b < nu[0], jnp.maximum(c - MOE_NCH, 0), last)

    grid_spec = pltpu.PrefetchScalarGridSpec(
        num_scalar_prefetch=4,
        grid=(MOE_NB, 2 * MOE_NCH),
        in_specs=[
            pl.BlockSpec((MOE_TM, HALF), lambda b, c, be, bx, ns, nu: (bx[b], 0)),
            pl.BlockSpec((1, D_MODEL, MOE_CH),
                         lambda b, c, be, bx, ns, nu: (be[b], 0, up_chunk(b, c, nu))),
            pl.BlockSpec((1, D_MODEL, MOE_CH),
                         lambda b, c, be, bx, ns, nu: (be[b], 0, MOE_NCH + up_chunk(b, c, nu))),
            pl.BlockSpec((1, D_FF, MOE_CH),
                         lambda b, c, be, bx, ns, nu: (be[b], 0, down_chunk(b, c, nu))),
            pl.BlockSpec((1, 1, MOE_CH),
                         lambda b, c, be, bx, ns, nu: (be[b], 0, up_chunk(b, c, nu))),
            pl.BlockSpec((1, 1, MOE_CH),
                         lambda b, c, be, bx, ns, nu: (be[b], 0, MOE_NCH + up_chunk(b, c, nu))),
            pl.BlockSpec((1, 1, MOE_CH),
                         lambda b, c, be, bx, ns, nu: (be[b], 0, down_chunk(b, c, nu))),
        ],
        out_specs=pl.BlockSpec((MOE_TM, MOE_CH),
                               lambda b, c, be, bx, ns, nu: (bx[b], down_chunk(b, c, nu))),
        scratch_shapes=[pltpu.VMEM((MOE_NCH, MOE_TM, MOE_CH), BF16)],
    )
    return pl.pallas_call(
        _moe_body,
        grid_spec=grid_spec,
        out_shape=jax.ShapeDtypeStruct((N_SLOTS, D_MODEL), F32),
        compiler_params=_cparams(("arbitrary", "arbitrary"), 60),
        name="moe",
    )(blk_expert, blk_x, blk_nsub, n_used, xs, w_gate_up, w_gate_up, w_down,
      b_gate_up, b_gate_up, b_down)


def _combine_body(dest_ref, gates_ref, h_ref, nw_ref, ys_ref, o_ref, buf_ref, sem):
    i = pl.program_id(0)
    nblk = pl.num_programs(0)

    def row_copy(blk, slot, k, t):
        d = dest_ref[k * SEQ + blk * COMBINE_TB + t]
        return pltpu.make_async_copy(ys_ref.at[pl.ds(d, 1)],
                                     buf_ref.at[slot, k, pl.ds(t, 1)], sem.at[slot])

    def start_block(blk, slot):
        def body(t, carry):
            for k in range(TOP_K):
                row_copy(blk, slot, k, t).start()
            return carry
        lax.fori_loop(0, COMBINE_TB, body, 0)

    def wait_block(blk, slot):
        def body(t, carry):
            for k in range(TOP_K):
                row_copy(blk, slot, k, t).wait()
            return carry
        lax.fori_loop(0, COMBINE_TB, body, 0)

    slot = i % 2

    @pl.when(i == 0)
    def _():
        start_block(0, 0)

    @pl.when(i + 1 < nblk)
    def _():
        start_block(i + 1, 1 - slot)

    wait_block(i, slot)

    acc = h_ref[...]
    for k in range(TOP_K):
        acc = acc + gates_ref[:, k:k + 1] * buf_ref[slot, k]
    ms = jnp.mean(acc * acc, axis=-1, keepdims=True)
    o_ref[...] = acc * lax.rsqrt(ms + NORM_EPS) * nw_ref[...]


def _combine(dest_flat, gates_t, h, final_norm_w, ys):
    grid_spec = pltpu.PrefetchScalarGridSpec(
        num_scalar_prefetch=1,
        grid=(SEQ // COMBINE_TB,),
        in_specs=[
            pl.BlockSpec((COMBINE_TB, TOP_K), lambda i, d: (i, 0)),
            pl.BlockSpec((COMBINE_TB, D_MODEL), lambda i, d: (i, 0)),
            pl.BlockSpec((1, D_MODEL), lambda i, d: (0, 0)),
            pl.BlockSpec(memory_space=pl.ANY),
        ],
        out_specs=pl.BlockSpec((COMBINE_TB, D_MODEL), lambda i, d: (i, 0)),
        scratch_shapes=[
            pltpu.VMEM((2, TOP_K, COMBINE_TB, D_MODEL), F32),
            pltpu.SemaphoreType.DMA((2,)),
        ],
    )
    return pl.pallas_call(
        _combine_body,
        grid_spec=grid_spec,
        out_shape=jax.ShapeDtypeStruct((SEQ, D_MODEL), F32),
        compiler_params=_cparams(("arbitrary",), 40),
        name="combine",
    )(dest_flat, gates_t, h, final_norm_w, ys)


def _routing_tables(counts, ids, ranks):
    nblk = (counts + MOE_TM - 1) // MOE_TM
    cum = jnp.cumsum(nblk)
    first_blk = cum - nblk
    n_used = cum[-1]
    first_of = jnp.zeros(ids.shape, I32)
    for e in range(N_EXPERTS):
        first_of = first_of + jnp.where(ids == e, first_blk[e], 0)
    dest = first_of * MOE_TM + ranks
    b = jnp.arange(MOE_NB, dtype=I32)
    valid = b < n_used
    blk_e = jnp.minimum(jnp.searchsorted(cum, b, side="right"), N_EXPERTS - 1).astype(I32)
    blk_e = jnp.where(valid, blk_e, blk_e[n_used - 1])
    rows = jnp.clip(counts[blk_e] - (b - first_blk[blk_e]) * MOE_TM, 0, MOE_TM)
    blk_nsub = jnp.where(valid, (rows + MOE_SUB - 1) // MOE_SUB, 0).astype(I32)
    blk_x = jnp.where(valid, b, n_used - 1).astype(I32)
    pad_start = (first_blk * MOE_TM + counts).astype(I32)
    pad_len = ((-counts) % MOE_SUB).astype(I32)
    return (dest.reshape(-1).astype(I32), blk_e, blk_x, blk_nsub,
            n_used.reshape(1).astype(I32), pad_start, pad_len)


def kernel(x, attn_norm_w, w_in, conv_w, rel_pos_bias, attn_out_norm_w, conv_out_norm_w, w_out,
           ffn_norm_w, router_w, router_b, w_gate_up, b_gate_up, w_down, b_down, final_norm_w):
    bsz, s, d = x.shape
    assert (bsz, s, d) == (1, SEQ, D_MODEL)
    assert attn_norm_w.shape[0] == 1
    x2d = x.reshape(SEQ, D_MODEL)

    proj = _inproj(x2d, attn_norm_w[0].reshape(1, D_MODEL), w_in[0])
    attn_n = _attention(proj, _bias_table(rel_pos_bias[0]), attn_out_norm_w[0].reshape(1, ATTN_WIDTH))
    conv_n = _conv(proj, conv_w[0], conv_out_norm_w[0].reshape(1, CONV_WIDTH))

    router_wt = router_w[0].T.astype(BF16)
    router_b_col = jnp.broadcast_to(router_b[0][:, None], (N_EXPERTS, V7X_LANES))
    h, hp, ids, gates, ranks, cnt = _outproj_router(
        attn_n, conv_n, w_out[0], x2d, ffn_norm_w[0].reshape(1, D_MODEL), router_wt, router_b_col)

    counts = cnt[:, 0].astype(I32)
    dest, blk_e, blk_x, blk_nsub, n_used, pad_start, pad_len = _routing_tables(
        counts, ids[:TOP_K], ranks[:TOP_K])

    xs = _dispatch(dest, pad_start, pad_len, hp)
    ys = _moe(blk_e, blk_x, blk_nsub, n_used, xs, w_gate_up[0], w_down[0],
              b_gate_up[0].reshape(N_EXPERTS, 1, 2 * D_FF), b_down[0].reshape(N_EXPERTS, 1, D_MODEL))
    out = _combine(dest, gates[:TOP_K].T, h, final_norm_w.reshape(1, D_MODEL), ys)
    return out.reshape(bsz, s, d)
```

```python
import functools
import math

import jax
import jax.numpy as jnp
from jax import lax
from jax.experimental import pallas as pl
from jax.experimental.pallas import tpu as pltpu

F32 = jnp.float32
BF16 = jnp.bfloat16
I32 = jnp.int32
U32 = jnp.uint32

D_MODEL = 2048
SEQ = 8192
HEAD_DIM = 128
NA_HEADS = 8
ATTN_WIDTH = NA_HEADS * HEAD_DIM
CONV_WIDTH = D_MODEL - ATTN_WIDTH
CONV_GROUPS = CONV_WIDTH // HEAD_DIM
N_PROJ = 3 * ATTN_WIDTH + 3 * CONV_WIDTH
GRID_W = 64
GRID_ROWS = SEQ // GRID_W
NA_KH = 8
NA_KW = 16
N_EXPERTS = 32
TOP_K = 4
D_FF = D_MODEL
SWIGLU_LIMIT = 7.0
SWIGLU_ALPHA = 1.702
NORM_EPS = 1e-5
NEG_INF = -1e30

V7X_LANES = 128
V7X_VMEM_BYTES = 64 * 1024 * 1024
MIB = 1024 * 1024

INPROJ_TM = 1024
INPROJ_TN = 1024
NORM_ROWS = 256
ATTN_ROWS_PER_STEP = 8
WIN_KEYS = NA_KH * GRID_W
CONV_CHUNK = 1024
OUT_TM = 1024
OUT_TN = 512
OUT_NCH = D_MODEL // OUT_TN
MOE_SUB = 256
MOE_TM = 5 * MOE_SUB
MOE_CH = 512
MOE_NCH = D_FF // MOE_CH
MOE_NB = (SEQ * TOP_K) // MOE_TM + 1 + N_EXPERTS - 1
N_SLOTS = MOE_NB * MOE_TM
HALF = D_MODEL // 2
DISPATCH_TB = 1024
COMBINE_TB = 256


def _cparams(semantics, vmem_mib):
    return pltpu.CompilerParams(dimension_semantics=semantics,
                                vmem_limit_bytes=vmem_mib * MIB)


def _inproj_body(x_ref, nw_ref, w_ref, o_ref, xn_ref):
    @pl.when(pl.program_id(1) == 0)
    def _():
        def chunk(i, carry):
            rows = pl.ds(pl.multiple_of(i * NORM_ROWS, NORM_ROWS), NORM_ROWS)
            x = x_ref[rows, :]
            ms = jnp.mean(x * x, axis=-1, keepdims=True)
            xn_ref[rows, :] = (x * lax.rsqrt(ms + NORM_EPS) * nw_ref[...]).astype(BF16)
            return carry
        lax.fori_loop(0, INPROJ_TM // NORM_ROWS, chunk, 0)

    o_ref[...] = jnp.dot(xn_ref[...], w_ref[...].astype(BF16),
                         preferred_element_type=F32).astype(BF16)


def _inproj(x2d, norm_w, w_in):
    return pl.pallas_call(
        _inproj_body,
        grid=(SEQ // INPROJ_TM, N_PROJ // INPROJ_TN),
        in_specs=[
            pl.BlockSpec((INPROJ_TM, D_MODEL), lambda m, n: (m, 0)),
            pl.BlockSpec((1, D_MODEL), lambda m, n: (0, 0)),
            pl.BlockSpec((D_MODEL, INPROJ_TN), lambda m, n: (0, n)),
        ],
        out_specs=pl.BlockSpec((INPROJ_TM, INPROJ_TN), lambda m, n: (m, n)),
        out_shape=jax.ShapeDtypeStruct((SEQ, N_PROJ), BF16),
        scratch_shapes=[pltpu.VMEM((INPROJ_TM, D_MODEL), BF16)],
        compiler_params=_cparams(("parallel", "arbitrary"), 56),
        name="inproj",
    )(x2d, norm_w, w_in)


def _bias_table(rel_pos_bias):
    cq = jnp.arange(GRID_W)[:, None]
    ck = jnp.arange(GRID_W)[None, :]
    cs = jnp.clip(cq - NA_KW // 2, 0, GRID_W - NA_KW)
    band = (ck >= cs) & (ck < cs + NA_KW)
    dc = jnp.clip(ck - cq, -(NA_KW - 1), NA_KW - 1) + (NA_KW - 1)
    t = jnp.zeros((NA_HEADS, 2 * NA_KH - 1, GRID_W, GRID_W), F32)
    for j in range(2 * NA_KW - 1):
        t = t + jnp.where(dc == j, rel_pos_bias[:, :, j, None, None], 0.0)
    t = jnp.where(band[None, None], t, NEG_INF)
    tb = jnp.stack([t[:, off:off + NA_KH] for off in range(NA_KH)], axis=1)
    return tb.transpose(0, 1, 3, 2, 4).reshape(NA_HEADS, NA_KH, GRID_W, WIN_KEYS).astype(F32)


def _attn_body(q_ref, k_ref, v_ref, tb_ref, nw_ref, o_ref):
    rb = pl.program_id(1)
    scale = 1.0 / math.sqrt(HEAD_DIM)
    for i in range(ATTN_ROWS_PER_STEP):
        r = rb * ATTN_ROWS_PER_STEP + i
        ws = jnp.clip(r - NA_KH // 2, 0, GRID_ROWS - NA_KH)
        off = ws - r + (NA_KH - 1)
        kstart = pl.multiple_of(ws * GRID_W, GRID_W)
        kk = k_ref[pl.ds(kstart, WIN_KEYS), :]
        vv = v_ref[pl.ds(kstart, WIN_KEYS), :]
        q = q_ref[i * GRID_W:(i + 1) * GRID_W, :]
        s = lax.dot_general(q, kk, (((1,), (1,)), ((), ())), preferred_element_type=F32)
        s = s * scale + tb_ref[0, off]
        m = jnp.max(s, axis=-1, keepdims=True)
        p = jnp.exp(s - m)
        l = jnp.sum(p, axis=-1, keepdims=True)
        o = jnp.dot(p.astype(BF16), vv, preferred_element_type=F32) / l
        ms = jnp.mean(o * o, axis=-1, keepdims=True)
        o_ref[i * GRID_W:(i + 1) * GRID_W, :] = (
            o * lax.rsqrt(ms + NORM_EPS) * nw_ref[...]).astype(BF16)


def _attention(proj, bias_tab, attn_out_norm_w):
    rows = ATTN_ROWS_PER_STEP * GRID_W
    return pl.pallas_call(
        _attn_body,
        grid=(NA_HEADS, GRID_ROWS // ATTN_ROWS_PER_STEP),
        in_specs=[
            pl.BlockSpec((rows, HEAD_DIM), lambda h, rb: (rb, h)),
            pl.BlockSpec((SEQ, HEAD_DIM), lambda h, rb: (0, NA_HEADS + h)),
            pl.BlockSpec((SEQ, HEAD_DIM), lambda h, rb: (0, 2 * NA_HEADS + h)),
            pl.BlockSpec((1, NA_KH, GRID_W, WIN_KEYS), lambda h, rb: (h, 0, 0, 0)),
            pl.BlockSpec((1, HEAD_DIM), lambda h, rb: (0, h)),
        ],
        out_specs=pl.BlockSpec((rows, HEAD_DIM), lambda h, rb: (rb, h)),
        out_shape=jax.ShapeDtypeStruct((SEQ, ATTN_WIDTH), BF16),
        compiler_params=_cparams(("parallel", "arbitrary"), 32),
        name="attn",
    )(proj, proj, proj, bias_tab, attn_out_norm_w)


CONV_PAD = 8


def _conv_body(u_ref, b_ref, c_ref, cw_ref, nw_ref, o_ref, z_ref):
    zeros = jnp.zeros((CONV_PAD, HEAD_DIM), F32)
    z_ref[0:CONV_PAD, :] = zeros
    z_ref[CONV_PAD + SEQ:2 * CONV_PAD + SEQ, :] = zeros
    for ch in range(SEQ // CONV_CHUNK):
        lo = ch * CONV_CHUNK
        z_ref[CONV_PAD + lo:CONV_PAD + lo + CONV_CHUNK, :] = (
            c_ref[lo:lo + CONV_CHUNK, :].astype(F32) * u_ref[lo:lo + CONV_CHUNK, :].astype(F32))
    w0 = cw_ref[0:1, :]
    w1 = cw_ref[1:2, :]
    w2 = cw_ref[2:3, :]
    for ch in range(SEQ // CONV_CHUNK):
        lo = ch * CONV_CHUNK
        base = CONV_PAD + lo
        conv = (w0 * z_ref[base - 1:base - 1 + CONV_CHUNK, :]
                + w1 * z_ref[base:base + CONV_CHUNK, :]
                + w2 * z_ref[base + 1:base + 1 + CONV_CHUNK, :])
        y = b_ref[lo:lo + CONV_CHUNK, :].astype(F32) * conv
        ms = jnp.mean(y * y, axis=-1, keepdims=True)
        o_ref[lo:lo + CONV_CHUNK, :] = (y * lax.rsqrt(ms + NORM_EPS) * nw_ref[...]).astype(BF16)


def _conv(proj, conv_w, conv_out_norm_w):
    col0 = 3 * NA_HEADS
    return pl.pallas_call(
        _conv_body,
        grid=(CONV_GROUPS,),
        in_specs=[
            pl.BlockSpec((SEQ, HEAD_DIM), lambda g: (0, col0 + g)),
            pl.BlockSpec((SEQ, HEAD_DIM), lambda g: (0, col0 + CONV_GROUPS + g)),
            pl.BlockSpec((SEQ, HEAD_DIM), lambda g: (0, col0 + 2 * CONV_GROUPS + g)),
            pl.BlockSpec((3, HEAD_DIM), lambda g: (0, g)),
            pl.BlockSpec((1, HEAD_DIM), lambda g: (0, g)),
        ],
        out_specs=pl.BlockSpec((SEQ, HEAD_DIM), lambda g: (0, g)),
        out_shape=jax.ShapeDtypeStruct((SEQ, CONV_WIDTH), BF16),
        scratch_shapes=[pltpu.VMEM((SEQ + 2 * CONV_PAD, HEAD_DIM), F32)],
        compiler_params=_cparams(("arbitrary",), 48),
        name="conv",
    )(proj, proj, proj, conv_w, conv_out_norm_w)


def _bf16_bits(x):
    return lax.bitcast_convert_type(x.astype(BF16).astype(F32), U32) & jnp.uint32(0xFFFF0000)


def _outproj_body(a_ref, c_ref, w_ref, x_ref, fw_ref, rwt_ref, rb_ref,
                  h_ref, hp_ref, ids_ref, gates_ref, ranks_ref, cnt_ref,
                  hacc_ref, carry_ref, tri_ref):
    m = pl.program_id(0)
    n = pl.program_id(1)

    @pl.when((m == 0) & (n == 0))
    def _():
        carry_ref[...] = jnp.zeros_like(carry_ref)
        r = lax.broadcasted_iota(I32, (OUT_TM, OUT_TM), 0)
        c = lax.broadcasted_iota(I32, (OUT_TM, OUT_TM), 1)
        tri_ref[...] = (r < c).astype(BF16)

    w = w_ref[...].astype(BF16)
    acc = (jnp.dot(a_ref[...], w[:ATTN_WIDTH], preferred_element_type=F32)
           + jnp.dot(c_ref[...], w[ATTN_WIDTH:], preferred_element_type=F32)
           + x_ref[...])
    h_ref[...] = acc
    hacc_ref[n] = acc

    @pl.when(n == OUT_NCH - 1)
    def _():
        ss = jnp.zeros((OUT_TM, 1), F32)
        for j in range(OUT_NCH):
            hj = hacc_ref[j]
            ss = ss + jnp.sum(hj * hj, axis=-1, keepdims=True)
        inv = lax.rsqrt(ss * (1.0 / D_MODEL) + NORM_EPS)
        hn = []
        logits = jnp.zeros((N_EXPERTS, OUT_TM), F32)
        for j in range(OUT_NCH):
            hj = (hacc_ref[j] * inv * fw_ref[:, j * OUT_TN:(j + 1) * OUT_TN]).astype(BF16)
            hn.append(hj)
            logits = logits + lax.dot_general(
                rwt_ref[:, j * OUT_TN:(j + 1) * OUT_TN], hj,
                (((1,), (1,)), ((), ())), preferred_element_type=F32)
        logits = logits + rb_ref[:, 0:1]
        half_chunks = OUT_NCH // 2
        for j in range(half_chunks):
            lo = _bf16_bits(hn[j].astype(F32)) >> 16
            hi = _bf16_bits(hn[j + half_chunks].astype(F32))
            hp_ref[:, j * OUT_TN:(j + 1) * OUT_TN] = lo | hi

        eio = lax.broadcasted_iota(I32, (N_EXPERTS, OUT_TM), 0)
        work = logits
        tops, idxs, hots = [], [], []
        for _ in range(TOP_K):
            mx = jnp.max(work, axis=0, keepdims=True)
            idx = jnp.min(jnp.where(work == mx, eio, N_EXPERTS), axis=0, keepdims=True)
            hot = eio == idx
            tops.append(mx)
            idxs.append(idx)
            hots.append(hot)
            work = jnp.where(hot, -jnp.inf, work)
        exps = [jnp.exp(t - tops[0]) for t in tops]
        den = exps[0] + exps[1] + exps[2] + exps[3]
        multi = jnp.zeros((N_EXPERTS, OUT_TM), F32)
        for hot in hots:
            multi = multi + hot.astype(F32)
        carry = carry_ref[:, 0:1]
        before = carry + jnp.dot(multi.astype(BF16), tri_ref[...], preferred_element_type=F32)
        zero_i = jnp.zeros((1, OUT_TM), I32)
        zero_f = jnp.zeros((1, OUT_TM), F32)
        for k in range(TOP_K):
            ids_ref[k:k + 1, :] = idxs[k]
            gates_ref[k:k + 1, :] = exps[k] / den
            rank = jnp.sum(jnp.where(hots[k], before, 0.0), axis=0, keepdims=True)
            ranks_ref[k:k + 1, :] = rank.astype(I32)
            ids_ref[TOP_K + k:TOP_K + k + 1, :] = zero_i
            gates_ref[TOP_K + k:TOP_K + k + 1, :] = zero_f
            ranks_ref[TOP_K + k:TOP_K + k + 1, :] = zero_i
        new_carry = carry + jnp.sum(multi, axis=1, keepdims=True)
        full = jnp.broadcast_to(new_carry, (N_EXPERTS, V7X_LANES))
        carry_ref[...] = full
        cnt_ref[...] = full


def _outproj_router(attn_n, conv_n, w_out, x2d, ffn_norm_w, router_wt, router_b_col):
    grid = (SEQ // OUT_TM, OUT_NCH)
    meta = lambda m, n: (0, m)
    return pl.pallas_call(
        _outproj_body,
        grid=grid,
        in_specs=[
            pl.BlockSpec((OUT_TM, ATTN_WIDTH), lambda m, n: (m, 0)),
            pl.BlockSpec((OUT_TM, CONV_WIDTH), lambda m, n: (m, 0)),
            pl.BlockSpec((D_MODEL, OUT_TN), lambda m, n: (0, n)),
            pl.BlockSpec((OUT_TM, OUT_TN), lambda m, n: (m, n)),
            pl.BlockSpec((1, D_MODEL), lambda m, n: (0, 0)),
            pl.BlockSpec((N_EXPERTS, D_MODEL), lambda m, n: (0, 0)),
            pl.BlockSpec((N_EXPERTS, V7X_LANES), lambda m, n: (0, 0)),
        ],
        out_specs=[
            pl.BlockSpec((OUT_TM, OUT_TN), lambda m, n: (m, n)),
            pl.BlockSpec((OUT_TM, HALF), lambda m, n: (m, 0)),
            pl.BlockSpec((2 * TOP_K, OUT_TM), meta),
            pl.BlockSpec((2 * TOP_K, OUT_TM), meta),
            pl.BlockSpec((2 * TOP_K, OUT_TM), meta),
            pl.BlockSpec((N_EXPERTS, V7X_LANES), lambda m, n: (0, 0)),
        ],
        out_shape=[
            jax.ShapeDtypeStruct((SEQ, D_MODEL), F32),
            jax.ShapeDtypeStruct((SEQ, HALF), U32),
            jax.ShapeDtypeStruct((2 * TOP_K, SEQ), I32),
            jax.ShapeDtypeStruct((2 * TOP_K, SEQ), F32),
            jax.ShapeDtypeStruct((2 * TOP_K, SEQ), I32),
            jax.ShapeDtypeStruct((N_EXPERTS, V7X_LANES), F32),
        ],
        scratch_shapes=[
            pltpu.VMEM((OUT_NCH, OUT_TM, OUT_TN), F32),
            pltpu.VMEM((N_EXPERTS, V7X_LANES), F32),
            pltpu.VMEM((OUT_TM, OUT_TM), BF16),
        ],
        compiler_params=_cparams(("arbitrary", "arbitrary"), 56),
        name="outproj_router",
    )(attn_n, conv_n, w_out, x2d, ffn_norm_w, router_wt, router_b_col)


HBM_ROW_TILE = 8
PAD_BITS = (128, 64, 32, 16, 8)


def _dispatch_body(dest_ref, pad_start_ref, pad_len_ref, hp_ref, xs_ref, zbuf_ref, sem, zsem):
    i = pl.program_id(0)

    def row_copy(t, d):
        return pltpu.make_async_copy(hp_ref.at[pl.ds(t, 1)], xs_ref.at[pl.ds(d, 1)], sem)

    def pad_head(e):
        return (-pad_start_ref[e]) & (HBM_ROW_TILE - 1)

    def pad_row_copy(e, j):
        return pltpu.make_async_copy(zbuf_ref.at[pl.ds(0, 1)],
                                     xs_ref.at[pl.ds(pad_start_ref[e] + j, 1)], zsem)

    def pad_piece_copy(e, bit):
        body = pad_len_ref[e] - pad_head(e)
        start = pad_start_ref[e] + pad_head(e) + (body & ~(2 * bit - 1))
        start = pl.multiple_of(start, HBM_ROW_TILE)
        return pltpu.make_async_copy(zbuf_ref.at[pl.ds(0, bit)], xs_ref.at[pl.ds(start, bit)], zsem)

    def for_each_pad(action):
        def per_expert(e, carry):
            for j in range(HBM_ROW_TILE - 1):
                @pl.when(j < pad_head(e))
                def _():
                    action(pad_row_copy(e, j))
            for bit in PAD_BITS:
                @pl.when(((pad_len_ref[e] - pad_head(e)) & bit) != 0)
                def _():
                    action(pad_piece_copy(e, bit))
            return carry
        lax.fori_loop(0, N_EXPERTS, per_expert, 0)

    @pl.when(i == 0)
    def _():
        zbuf_ref[...] = jnp.zeros_like(zbuf_ref)
        for_each_pad(lambda cp: cp.start())
        for_each_pad(lambda cp: cp.wait())

    def start_rows(t, carry):
        tok = i * DISPATCH_TB + t
        for k in range(TOP_K):
            row_copy(t, dest_ref[k * SEQ + tok]).start(priority=k % 2)
        return carry
    lax.fori_loop(0, DISPATCH_TB, start_rows, 0, unroll=4)

    def wait_rows(t, carry):
        tok = i * DISPATCH_TB + t
        for k in range(TOP_K):
            row_copy(t, dest_ref[k * SEQ + tok]).wait()
        return carry
    lax.fori_loop(0, DISPATCH_TB, wait_rows, 0, unroll=4)


def _dispatch(dest_flat, pad_start, pad_len, hp):
    grid_spec = pltpu.PrefetchScalarGridSpec(
        num_scalar_prefetch=3,
        grid=(SEQ // DISPATCH_TB,),
        in_specs=[pl.BlockSpec((DISPATCH_TB, HALF), lambda i, d, ps, pn: (i, 0))],
        out_specs=pl.BlockSpec(memory_space=pl.ANY),
        scratch_shapes=[
            pltpu.VMEM((PAD_BITS[0], HALF), U32),
            pltpu.SemaphoreType.DMA(()),
            pltpu.SemaphoreType.DMA(()),
        ],
    )
    return pl.pallas_call(
        _dispatch_body,
        grid_spec=grid_spec,
        out_shape=jax.ShapeDtypeStruct((N_SLOTS, HALF), U32),
        compiler_params=pltpu.CompilerParams(dimension_semantics=("arbitrary",),
                                             has_side_effects=True),
        name="dispatch",
    )(dest_flat, pad_start, pad_len, hp)


def _unpack_rows(words):
    lo = lax.bitcast_convert_type(words << 16, F32).astype(BF16)
    hi = lax.bitcast_convert_type(words & jnp.uint32(0xFFFF0000), F32).astype(BF16)
    return lo, hi


MOE_PASS_WIDTHS = (4, 2, 1)
assert MOE_TM // MOE_SUB < 2 * MOE_PASS_WIDTHS[0]


def _for_each_pass(nsub, fn):
    for width in MOE_PASS_WIDTHS:
        @pl.when((nsub & width) != 0)
        def _():
            first = (nsub & ~(2 * width - 1)) * MOE_SUB
            fn(pl.multiple_of(first, MOE_SUB), width * MOE_SUB)


def _moe_body(be_ref, bx_ref, ns_ref, nu_ref,
              x_ref, wg_ref, wl_ref, wd_ref, bg_ref, bl_ref, bd_ref,
              o_ref, act_ref):
    b = pl.program_id(0)
    c = pl.program_id(1)
    valid = b < nu_ref[0]
    nsub = ns_ref[b]

    @pl.when(valid & (c < MOE_NCH))
    def _():
        def up(first, n_rows):
            rows = pl.ds(first, n_rows)
            lo, hi = _unpack_rows(x_ref[rows, :])
            g = (jnp.dot(lo, wg_ref[0, :HALF, :].astype(BF16), preferred_element_type=F32)
                 + jnp.dot(hi, wg_ref[0, HALF:, :].astype(BF16), preferred_element_type=F32)
                 + bg_ref[0])
            lin = (jnp.dot(lo, wl_ref[0, :HALF, :].astype(BF16), preferred_element_type=F32)
                   + jnp.dot(hi, wl_ref[0, HALF:, :].astype(BF16), preferred_element_type=F32)
                   + bl_ref[0])
            g = jnp.minimum(g, SWIGLU_LIMIT)
            lin = jnp.clip(lin, -SWIGLU_LIMIT, SWIGLU_LIMIT)
            act = g * jax.nn.sigmoid(SWIGLU_ALPHA * g) * (lin + 1.0)
            act_ref[c, rows, :] = act.astype(BF16)
        _for_each_pass(nsub, up)

    @pl.when(valid & (c >= MOE_NCH))
    def _():
        def down(first, n_rows):
            rows = pl.ds(first, n_rows)
            y = jnp.zeros((n_rows, MOE_CH), F32) + bd_ref[0]
            for k in range(MOE_NCH):
                y = y + jnp.dot(act_ref[k, rows, :],
                                wd_ref[0, k * MOE_CH:(k + 1) * MOE_CH, :].astype(BF16),
                                preferred_element_type=F32)
            o_ref[rows, :] = y
        _for_each_pass(nsub, down)

        def fill(i, carry):
            rows = pl.ds(pl.multiple_of(i * MOE_SUB, MOE_SUB), MOE_SUB)
            o_ref[rows, :] = jnp.zeros((MOE_SUB, MOE_CH), F32)
            return carry
        lax.fori_loop(nsub, MOE_TM // MOE_SUB, fill, 0)


def _moe(blk_expert, blk_x, blk_nsub, n_used, xs, w_gate_up, w_down, b_gate_up, b_down):
    last = MOE_NCH - 1

    def up_chunk(b, c, nu):
        return jnp.where(b < nu[0], jnp.minimum(c, last), last)

    def down_chunk(b, c, nu):
        return jnp.where(b < nu[0], jnp.maximum(c - MOE_NCH, 0), last)

    grid_spec = pltpu.PrefetchScalarGridSpec(
        num_scalar_prefetch=4,
        grid=(MOE_NB, 2 * MOE_NCH),
        in_specs=[
            pl.BlockSpec((MOE_TM, HALF), lambda b, c, be, bx, ns, nu: (bx[b], 0)),
            pl.BlockSpec((1, D_MODEL, MOE_CH),
                         lambda b, c, be, bx, ns, nu: (be[b], 0, up_chunk(b, c, nu))),
            pl.BlockSpec((1, D_MODEL, MOE_CH),
                         lambda b, c, be, bx, ns, nu: (be[b], 0, MOE_NCH + up_chunk(b, c, nu))),
            pl.BlockSpec((1, D_FF, MOE_CH),
                         lambda b, c, be, bx, ns, nu: (be[b], 0, down_chunk(b, c, nu))),
            pl.BlockSpec((1, 1, MOE_CH),
                         lambda b, c, be, bx, ns, nu: (be[b], 0, up_chunk(b, c, nu))),
            pl.BlockSpec((1, 1, MOE_CH),
                         lambda b, c, be, bx, ns, nu: (be[b], 0, MOE_NCH + up_chunk(b, c, nu))),
            pl.BlockSpec((1, 1, MOE_CH),
                         lambda b, c, be, bx, ns, nu: (be[b], 0, down_chunk(b, c, nu))),
        ],
        out_specs=pl.BlockSpec((MOE_TM, MOE_CH),
                               lambda b, c, be, bx, ns, nu: (bx[b], down_chunk(b, c, nu))),
        scratch_shapes=[pltpu.VMEM((MOE_NCH, MOE_TM, MOE_CH), BF16)],
    )
    return pl.pallas_call(
        _moe_body,
        grid_spec=grid_spec,
        out_shape=jax.ShapeDtypeStruct((N_SLOTS, D_MODEL), F32),
        compiler_params=_cparams(("arbitrary", "arbitrary"), 60),
        name="moe",
    )(blk_expert, blk_x, blk_nsub, n_used, xs, w_gate_up, w_gate_up, w_down,
      b_gate_up, b_gate_up, b_down)


def _combine_body(dest_ref, gates_ref, h_ref, nw_ref, ys_ref, o_ref, buf_ref, sem):
    i = pl.program_id(0)
    nblk = pl.num_programs(0)

    def row_copy(blk, slot, k, t):
        d = dest_ref[k * SEQ + blk * COMBINE_TB + t]
        return pltpu.make_async_copy(ys_ref.at[pl.ds(d, 1)],
                                     buf_ref.at[slot, k, pl.ds(t, 1)], sem.at[slot])

    def start_block(blk, slot):
        def body(t, carry):
            for k in range(TOP_K):
                row_copy(blk, slot, k, t).start(priority=k % 2)
            return carry
        lax.fori_loop(0, COMBINE_TB, body, 0)

    def wait_block(blk, slot):
        def body(t, carry):
            for k in range(TOP_K):
                row_copy(blk, slot, k, t).wait()
            return carry
        lax.fori_loop(0, COMBINE_TB, body, 0)

    slot = i % 2

    @pl.when(i == 0)
    def _():
        start_block(0, 0)

    @pl.when(i + 1 < nblk)
    def _():
        start_block(i + 1, 1 - slot)

    wait_block(i, slot)

    acc = h_ref[...]
    for k in range(TOP_K):
        acc = acc + gates_ref[:, k:k + 1] * buf_ref[slot, k]
    ms = jnp.mean(acc * acc, axis=-1, keepdims=True)
    o_ref[...] = acc * lax.rsqrt(ms + NORM_EPS) * nw_ref[...]


def _combine(dest_flat, gates_t, h, final_norm_w, ys):
    grid_spec = pltpu.PrefetchScalarGridSpec(
        num_scalar_prefetch=1,
        grid=(SEQ // COMBINE_TB,),
        in_specs=[
            pl.BlockSpec((COMBINE_TB, TOP_K), lambda i, d: (i, 0)),
            pl.BlockSpec((COMBINE_TB, D_MODEL), lambda i, d: (i, 0)),
            pl.BlockSpec((1, D_MODEL), lambda i, d: (0, 0)),
            pl.BlockSpec(memory_space=pl.ANY),
        ],
        out_specs=pl.BlockSpec((COMBINE_TB, D_MODEL), lambda i, d: (i, 0)),
        scratch_shapes=[
            pltpu.VMEM((2, TOP_K, COMBINE_TB, D_MODEL), F32),
            pltpu.SemaphoreType.DMA((2,)),
        ],
    )
    return pl.pallas_call(
        _combine_body,
        grid_spec=grid_spec,
        out_shape=jax.ShapeDtypeStruct((SEQ, D_MODEL), F32),
        compiler_params=_cparams(("arbitrary",), 40),
        name="combine",
    )(dest_flat, gates_t, h, final_norm_w, ys)


def _routing_tables(counts, ids, ranks):
    nblk = (counts + MOE_TM - 1) // MOE_TM
    cum = jnp.cumsum(nblk)
    first_blk = cum - nblk
    n_used = cum[-1]
    first_of = jnp.zeros(ids.shape, I32)
    for e in range(N_EXPERTS):
        first_of = first_of + jnp.where(ids == e, first_blk[e], 0)
    dest = first_of * MOE_TM + ranks
    b = jnp.arange(MOE_NB, dtype=I32)
    valid = b < n_used
    blk_e = jnp.minimum(jnp.searchsorted(cum, b, side="right"), N_EXPERTS - 1).astype(I32)
    blk_e = jnp.where(valid, blk_e, blk_e[n_used - 1])
    rows = jnp.clip(counts[blk_e] - (b - first_blk[blk_e]) * MOE_TM, 0, MOE_TM)
    blk_nsub = jnp.where(valid, (rows + MOE_SUB - 1) // MOE_SUB, 0).astype(I32)
    blk_x = jnp.where(valid, b, n_used - 1).astype(I32)
    pad_start = (first_blk * MOE_TM + counts).astype(I32)
    pad_len = ((-counts) % MOE_SUB).astype(I32)
    return (dest.reshape(-1).astype(I32), blk_e, blk_x, blk_nsub,
            n_used.reshape(1).astype(I32), pad_start, pad_len)


def kernel(x, attn_norm_w, w_in, conv_w, rel_pos_bias, attn_out_norm_w, conv_out_norm_w, w_out,
           ffn_norm_w, router_w, router_b, w_gate_up, b_gate_up, w_down, b_down, final_norm_w):
    bsz, s, d = x.shape
    assert (bsz, s, d) == (1, SEQ, D_MODEL)
    assert attn_norm_w.shape[0] == 1
    x2d = x.reshape(SEQ, D_MODEL)

    proj = _inproj(x2d, attn_norm_w[0].reshape(1, D_MODEL), w_in[0])
    attn_n = _attention(proj, _bias_table(rel_pos_bias[0]), attn_out_norm_w[0].reshape(1, ATTN_WIDTH))
    conv_n = _conv(proj, conv_w[0], conv_out_norm_w[0].reshape(1, CONV_WIDTH))

    router_wt = router_w[0].T.astype(BF16)
    router_b_col = jnp.broadcast_to(router_b[0][:, None], (N_EXPERTS, V7X_LANES))
    h, hp, ids, gates, ranks, cnt = _outproj_router(
        attn_n, conv_n, w_out[0], x2d, ffn_norm_w[0].reshape(1, D_MODEL), router_wt, router_b_col)

    counts = cnt[:, 0].astype(I32)
    dest, blk_e, blk_x, blk_nsub, n_used, pad_start, pad_len = _routing_tables(
        counts, ids[:TOP_K], ranks[:TOP_K])

    xs = _dispatch(dest, pad_start, pad_len, hp)
    ys = _moe(blk_e, blk_x, blk_nsub, n_used, xs, w_gate_up[0], w_down[0],
              b_gate_up[0].reshape(N_EXPERTS, 1, 2 * D_FF), b_down[0].reshape(N_EXPERTS, 1, D_MODEL))
    out = _combine(dest, gates[:TOP_K].T, h, final_norm_w.reshape(1, D_MODEL), ys)
    return out.reshape(bsz, s, d)
```

```python
import functools
import math

import jax
import jax.numpy as jnp
from jax import lax
from jax.experimental import pallas as pl
from jax.experimental.pallas import tpu as pltpu

F32 = jnp.float32
BF16 = jnp.bfloat16
I32 = jnp.int32
U32 = jnp.uint32

D_MODEL = 2048
SEQ = 8192
HEAD_DIM = 128
NA_HEADS = 8
ATTN_WIDTH = NA_HEADS * HEAD_DIM
CONV_WIDTH = D_MODEL - ATTN_WIDTH
CONV_GROUPS = CONV_WIDTH // HEAD_DIM
N_PROJ = 3 * ATTN_WIDTH + 3 * CONV_WIDTH
GRID_W = 64
GRID_ROWS = SEQ // GRID_W
NA_KH = 8
NA_KW = 16
N_EXPERTS = 32
TOP_K = 4
D_FF = D_MODEL
SWIGLU_LIMIT = 7.0
SWIGLU_ALPHA = 1.702
NORM_EPS = 1e-5
NEG_INF = -1e30

V7X_LANES = 128
V7X_VMEM_BYTES = 64 * 1024 * 1024
MIB = 1024 * 1024

INPROJ_TM = 1024
INPROJ_TN = 1024
NORM_ROWS = 256
ATTN_ROWS_PER_STEP = 16
WIN_KEYS = NA_KH * GRID_W
CONV_CHUNK = 1024
OUT_TM = 1024
OUT_TN = 512
OUT_NCH = D_MODEL // OUT_TN
MOE_SUB = 256
MOE_TM = 5 * MOE_SUB
MOE_CH = 512
MOE_NCH = D_FF // MOE_CH
MOE_NB = (SEQ * TOP_K) // MOE_TM + 1 + N_EXPERTS - 1
N_SLOTS = MOE_NB * MOE_TM
HALF = D_MODEL // 2
DISPATCH_TB = 1024
COMBINE_TB = 256


def _cparams(semantics, vmem_mib):
    return pltpu.CompilerParams(dimension_semantics=semantics,
                                vmem_limit_bytes=vmem_mib * MIB)


def _inproj_body(x_ref, nw_ref, w_ref, o_ref, xn_ref):
    @pl.when(pl.program_id(1) == 0)
    def _():
        def chunk(i, carry):
            rows = pl.ds(pl.multiple_of(i * NORM_ROWS, NORM_ROWS), NORM_ROWS)
            x = x_ref[rows, :]
            ms = jnp.mean(x * x, axis=-1, keepdims=True)
            xn_ref[rows, :] = (x * lax.rsqrt(ms + NORM_EPS) * nw_ref[...]).astype(BF16)
            return carry
        lax.fori_loop(0, INPROJ_TM // NORM_ROWS, chunk, 0)

    o_ref[...] = jnp.dot(xn_ref[...], w_ref[...].astype(BF16),
                         preferred_element_type=F32).astype(BF16)


def _inproj(x2d, norm_w, w_in):
    return pl.pallas_call(
        _inproj_body,
        grid=(SEQ // INPROJ_TM, N_PROJ // INPROJ_TN),
        in_specs=[
            pl.BlockSpec((INPROJ_TM, D_MODEL), lambda m, n: (m, 0)),
            pl.BlockSpec((1, D_MODEL), lambda m, n: (0, 0)),
            pl.BlockSpec((D_MODEL, INPROJ_TN), lambda m, n: (0, n)),
        ],
        out_specs=pl.BlockSpec((INPROJ_TM, INPROJ_TN), lambda m, n: (m, n)),
        out_shape=jax.ShapeDtypeStruct((SEQ, N_PROJ), BF16),
        scratch_shapes=[pltpu.VMEM((INPROJ_TM, D_MODEL), BF16)],
        compiler_params=_cparams(("parallel", "arbitrary"), 56),
        name="inproj",
    )(x2d, norm_w, w_in)


def _bias_table(rel_pos_bias):
    cq = jnp.arange(GRID_W)[:, None]
    ck = jnp.arange(GRID_W)[None, :]
    cs = jnp.clip(cq - NA_KW // 2, 0, GRID_W - NA_KW)
    band = (ck >= cs) & (ck < cs + NA_KW)
    dc = jnp.clip(ck - cq, -(NA_KW - 1), NA_KW - 1) + (NA_KW - 1)
    t = jnp.zeros((NA_HEADS, 2 * NA_KH - 1, GRID_W, GRID_W), F32)
    for j in range(2 * NA_KW - 1):
        t = t + jnp.where(dc == j, rel_pos_bias[:, :, j, None, None], 0.0)
    t = jnp.where(band[None, None], t, NEG_INF)
    tb = jnp.stack([t[:, off:off + NA_KH] for off in range(NA_KH)], axis=1)
    return tb.transpose(0, 1, 3, 2, 4).reshape(NA_HEADS, NA_KH, GRID_W, WIN_KEYS).astype(F32)


def _attn_body(q_ref, k_ref, v_ref, tb_ref, nw_ref, o_ref):
    rb = pl.program_id(1)
    scale = 1.0 / math.sqrt(HEAD_DIM)
    for i in range(ATTN_ROWS_PER_STEP):
        r = rb * ATTN_ROWS_PER_STEP + i
        ws = jnp.clip(r - NA_KH // 2, 0, GRID_ROWS - NA_KH)
        off = ws - r + (NA_KH - 1)
        kstart = pl.multiple_of(ws * GRID_W, GRID_W)
        kk = k_ref[pl.ds(kstart, WIN_KEYS), :]
        vv = v_ref[pl.ds(kstart, WIN_KEYS), :]
        q = q_ref[i * GRID_W:(i + 1) * GRID_W, :]
        s = lax.dot_general(q, kk, (((1,), (1,)), ((), ())), preferred_element_type=F32)
        s = s * scale + tb_ref[0, off]
        m = jnp.max(s, axis=-1, keepdims=True)
        p = jnp.exp(s - m)
        l = jnp.sum(p, axis=-1, keepdims=True)
        o = jnp.dot(p.astype(BF16), vv, preferred_element_type=F32) / l
        ms = jnp.mean(o * o, axis=-1, keepdims=True)
        o_ref[i * GRID_W:(i + 1) * GRID_W, :] = (
            o * lax.rsqrt(ms + NORM_EPS) * nw_ref[...]).astype(BF16)


def _attention(proj, bias_tab, attn_out_norm_w):
    rows = ATTN_ROWS_PER_STEP * GRID_W
    return pl.pallas_call(
        _attn_body,
        grid=(NA_HEADS, GRID_ROWS // ATTN_ROWS_PER_STEP),
        in_specs=[
            pl.BlockSpec((rows, HEAD_DIM), lambda h, rb: (rb, h)),
            pl.BlockSpec((SEQ, HEAD_DIM), lambda h, rb: (0, NA_HEADS + h)),
            pl.BlockSpec((SEQ, HEAD_DIM), lambda h, rb: (0, 2 * NA_HEADS + h)),
            pl.BlockSpec((1, NA_KH, GRID_W, WIN_KEYS), lambda h, rb: (h, 0, 0, 0)),
            pl.BlockSpec((1, HEAD_DIM), lambda h, rb: (0, h)),
        ],
        out_specs=pl.BlockSpec((rows, HEAD_DIM), lambda h, rb: (rb, h)),
        out_shape=jax.ShapeDtypeStruct((SEQ, ATTN_WIDTH), BF16),
        compiler_params=_cparams(("parallel", "arbitrary"), 32),
        name="attn",
    )(proj, proj, proj, bias_tab, attn_out_norm_w)


CONV_PAD = 8


def _conv_body(u_ref, b_ref, c_ref, cw_ref, nw_ref, o_ref, z_ref):
    zeros = jnp.zeros((CONV_PAD, HEAD_DIM), F32)
    z_ref[0:CONV_PAD, :] = zeros
    z_ref[CONV_PAD + SEQ:2 * CONV_PAD + SEQ, :] = zeros
    for ch in range(SEQ // CONV_CHUNK):
        lo = ch * CONV_CHUNK
        z_ref[CONV_PAD + lo:CONV_PAD + lo + CONV_CHUNK, :] = (
            c_ref[lo:lo + CONV_CHUNK, :].astype(F32) * u_ref[lo:lo + CONV_CHUNK, :].astype(F32))
    w0 = cw_ref[0:1, :]
    w1 = cw_ref[1:2, :]
    w2 = cw_ref[2:3, :]
    for ch in range(SEQ // CONV_CHUNK):
        lo = ch * CONV_CHUNK
        base = CONV_PAD + lo
        conv = (w0 * z_ref[base - 1:base - 1 + CONV_CHUNK, :]
                + w1 * z_ref[base:base + CONV_CHUNK, :]
                + w2 * z_ref[base + 1:base + 1 + CONV_CHUNK, :])
        y = b_ref[lo:lo + CONV_CHUNK, :].astype(F32) * conv
        ms = jnp.mean(y * y, axis=-1, keepdims=True)
        o_ref[lo:lo + CONV_CHUNK, :] = (y * lax.rsqrt(ms + NORM_EPS) * nw_ref[...]).astype(BF16)


def _conv(proj, conv_w, conv_out_norm_w):
    col0 = 3 * NA_HEADS
    return pl.pallas_call(
        _conv_body,
        grid=(CONV_GROUPS,),
        in_specs=[
            pl.BlockSpec((SEQ, HEAD_DIM), lambda g: (0, col0 + g)),
            pl.BlockSpec((SEQ, HEAD_DIM), lambda g: (0, col0 + CONV_GROUPS + g)),
            pl.BlockSpec((SEQ, HEAD_DIM), lambda g: (0, col0 + 2 * CONV_GROUPS + g)),
            pl.BlockSpec((3, HEAD_DIM), lambda g: (0, g)),
            pl.BlockSpec((1, HEAD_DIM), lambda g: (0, g)),
        ],
        out_specs=pl.BlockSpec((SEQ, HEAD_DIM), lambda g: (0, g)),
        out_shape=jax.ShapeDtypeStruct((SEQ, CONV_WIDTH), BF16),
        scratch_shapes=[pltpu.VMEM((SEQ + 2 * CONV_PAD, HEAD_DIM), F32)],
        compiler_params=_cparams(("arbitrary",), 48),
        name="conv",
    )(proj, proj, proj, conv_w, conv_out_norm_w)


def _bf16_bits(x):
    return lax.bitcast_convert_type(x.astype(BF16).astype(F32), U32) & jnp.uint32(0xFFFF0000)


def _outproj_body(a_ref, c_ref, w_ref, x_ref, fw_ref, rwt_ref, rb_ref,
                  h_ref, hp_ref, ids_ref, gates_ref, ranks_ref, cnt_ref,
                  hacc_ref, carry_ref, tri_ref):
    m = pl.program_id(0)
    n = pl.program_id(1)

    @pl.when((m == 0) & (n == 0))
    def _():
        carry_ref[...] = jnp.zeros_like(carry_ref)
        r = lax.broadcasted_iota(I32, (OUT_TM, OUT_TM), 0)
        c = lax.broadcasted_iota(I32, (OUT_TM, OUT_TM), 1)
        tri_ref[...] = (r < c).astype(BF16)

    w = w_ref[...].astype(BF16)
    acc = (jnp.dot(a_ref[...], w[:ATTN_WIDTH], preferred_element_type=F32)
           + jnp.dot(c_ref[...], w[ATTN_WIDTH:], preferred_element_type=F32)
           + x_ref[...])
    h_ref[...] = acc
    hacc_ref[n] = acc

    @pl.when(n == OUT_NCH - 1)
    def _():
        ss = jnp.zeros((OUT_TM, 1), F32)
        for j in range(OUT_NCH):
            hj = hacc_ref[j]
            ss = ss + jnp.sum(hj * hj, axis=-1, keepdims=True)
        inv = lax.rsqrt(ss * (1.0 / D_MODEL) + NORM_EPS)
        hn = []
        logits = jnp.zeros((N_EXPERTS, OUT_TM), F32)
        for j in range(OUT_NCH):
            hj = (hacc_ref[j] * inv * fw_ref[:, j * OUT_TN:(j + 1) * OUT_TN]).astype(BF16)
            hn.append(hj)
            logits = logits + lax.dot_general(
                rwt_ref[:, j * OUT_TN:(j + 1) * OUT_TN], hj,
                (((1,), (1,)), ((), ())), preferred_element_type=F32)
        logits = logits + rb_ref[:, 0:1]
        half_chunks = OUT_NCH // 2
        for j in range(half_chunks):
            lo = _bf16_bits(hn[j].astype(F32)) >> 16
            hi = _bf16_bits(hn[j + half_chunks].astype(F32))
            hp_ref[:, j * OUT_TN:(j + 1) * OUT_TN] = lo | hi

        eio = lax.broadcasted_iota(I32, (N_EXPERTS, OUT_TM), 0)
        work = logits
        tops, idxs, hots = [], [], []
        for _ in range(TOP_K):
            mx = jnp.max(work, axis=0, keepdims=True)
            idx = jnp.min(jnp.where(work == mx, eio, N_EXPERTS), axis=0, keepdims=True)
            hot = eio == idx
            tops.append(mx)
            idxs.append(idx)
            hots.append(hot)
            work = jnp.where(hot, -jnp.inf, work)
        exps = [jnp.exp(t - tops[0]) for t in tops]
        den = exps[0] + exps[1] + exps[2] + exps[3]
        multi = jnp.zeros((N_EXPERTS, OUT_TM), F32)
        for hot in hots:
            multi = multi + hot.astype(F32)
        carry = carry_ref[:, 0:1]
        before = carry + jnp.dot(multi.astype(BF16), tri_ref[...], preferred_element_type=F32)
        zero_i = jnp.zeros((1, OUT_TM), I32)
        zero_f = jnp.zeros((1, OUT_TM), F32)
        for k in range(TOP_K):
            ids_ref[k:k + 1, :] = idxs[k]
            gates_ref[k:k + 1, :] = exps[k] / den
            rank = jnp.sum(jnp.where(hots[k], before, 0.0), axis=0, keepdims=True)
            ranks_ref[k:k + 1, :] = rank.astype(I32)
            ids_ref[TOP_K + k:TOP_K + k + 1, :] = zero_i
            gates_ref[TOP_K + k:TOP_K + k + 1, :] = zero_f
            ranks_ref[TOP_K + k:TOP_K + k + 1, :] = zero_i
        new_carry = carry + jnp.sum(multi, axis=1, keepdims=True)
        full = jnp.broadcast_to(new_carry, (N_EXPERTS, V7X_LANES))
        carry_ref[...] = full
        cnt_ref[...] = full


def _outproj_router(attn_n, conv_n, w_out, x2d, ffn_norm_w, router_wt, router_b_col):
    grid = (SEQ // OUT_TM, OUT_NCH)
    meta = lambda m, n: (0, m)
    return pl.pallas_call(
        _outproj_body,
        grid=grid,
        in_specs=[
            pl.BlockSpec((OUT_TM, ATTN_WIDTH), lambda m, n: (m, 0)),
            pl.BlockSpec((OUT_TM, CONV_WIDTH), lambda m, n: (m, 0)),
            pl.BlockSpec((D_MODEL, OUT_TN), lambda m, n: (0, n)),
            pl.BlockSpec((OUT_TM, OUT_TN), lambda m, n: (m, n)),
            pl.BlockSpec((1, D_MODEL), lambda m, n: (0, 0)),
            pl.BlockSpec((N_EXPERTS, D_MODEL), lambda m, n: (0, 0)),
            pl.BlockSpec((N_EXPERTS, V7X_LANES), lambda m, n: (0, 0)),
        ],
        out_specs=[
            pl.BlockSpec((OUT_TM, OUT_TN), lambda m, n: (m, n)),
            pl.BlockSpec((OUT_TM, HALF), lambda m, n: (m, 0)),
            pl.BlockSpec((2 * TOP_K, OUT_TM), meta),
            pl.BlockSpec((2 * TOP_K, OUT_TM), meta),
            pl.BlockSpec((2 * TOP_K, OUT_TM), meta),
            pl.BlockSpec((N_EXPERTS, V7X_LANES), lambda m, n: (0, 0)),
        ],
        out_shape=[
            jax.ShapeDtypeStruct((SEQ, D_MODEL), F32),
            jax.ShapeDtypeStruct((SEQ, HALF), U32),
            jax.ShapeDtypeStruct((2 * TOP_K, SEQ), I32),
            jax.ShapeDtypeStruct((2 * TOP_K, SEQ), F32),
            jax.ShapeDtypeStruct((2 * TOP_K, SEQ), I32),
            jax.ShapeDtypeStruct((N_EXPERTS, V7X_LANES), F32),
        ],
        scratch_shapes=[
            pltpu.VMEM((OUT_NCH, OUT_TM, OUT_TN), F32),
            pltpu.VMEM((N_EXPERTS, V7X_LANES), F32),
            pltpu.VMEM((OUT_TM, OUT_TM), BF16),
        ],
        compiler_params=_cparams(("arbitrary", "arbitrary"), 56),
        name="outproj_router",
    )(attn_n, conv_n, w_out, x2d, ffn_norm_w, router_wt, router_b_col)


HBM_ROW_TILE = 8
PAD_BITS = (128, 64, 32, 16, 8)


def _dispatch_body(dest_ref, pad_start_ref, pad_len_ref, hp_ref, xs_ref, zbuf_ref, sem, zsem):
    i = pl.program_id(0)

    def row_copy(t, d):
        return pltpu.make_async_copy(hp_ref.at[pl.ds(t, 1)], xs_ref.at[pl.ds(d, 1)], sem)

    def pad_head(e):
        return (-pad_start_ref[e]) & (HBM_ROW_TILE - 1)

    def pad_row_copy(e, j):
        return pltpu.make_async_copy(zbuf_ref.at[pl.ds(0, 1)],
                                     xs_ref.at[pl.ds(pad_start_ref[e] + j, 1)], zsem)

    def pad_piece_copy(e, bit):
        body = pad_len_ref[e] - pad_head(e)
        start = pad_start_ref[e] + pad_head(e) + (body & ~(2 * bit - 1))
        start = pl.multiple_of(start, HBM_ROW_TILE)
        return pltpu.make_async_copy(zbuf_ref.at[pl.ds(0, bit)], xs_ref.at[pl.ds(start, bit)], zsem)

    def for_each_pad(action):
        def per_expert(e, carry):
            for j in range(HBM_ROW_TILE - 1):
                @pl.when(j < pad_head(e))
                def _():
                    action(pad_row_copy(e, j))
            for bit in PAD_BITS:
                @pl.when(((pad_len_ref[e] - pad_head(e)) & bit) != 0)
                def _():
                    action(pad_piece_copy(e, bit))
            return carry
        lax.fori_loop(0, N_EXPERTS, per_expert, 0)

    @pl.when(i == 0)
    def _():
        zbuf_ref[...] = jnp.zeros_like(zbuf_ref)
        for_each_pad(lambda cp: cp.start())
        for_each_pad(lambda cp: cp.wait())

    def start_rows(t, carry):
        tok = i * DISPATCH_TB + t
        for k in range(TOP_K):
            row_copy(t, dest_ref[k * SEQ + tok]).start(priority=k % 2)
        return carry
    lax.fori_loop(0, DISPATCH_TB, start_rows, 0, unroll=4)

    def wait_rows(t, carry):
        tok = i * DISPATCH_TB + t
        for k in range(TOP_K):
            row_copy(t, dest_ref[k * SEQ + tok]).wait()
        return carry
    lax.fori_loop(0, DISPATCH_TB, wait_rows, 0, unroll=4)


def _dispatch(dest_flat, pad_start, pad_len, hp):
    grid_spec = pltpu.PrefetchScalarGridSpec(
        num_scalar_prefetch=3,
        grid=(SEQ // DISPATCH_TB,),
        in_specs=[pl.BlockSpec((DISPATCH_TB, HALF), lambda i, d, ps, pn: (i, 0))],
        out_specs=pl.BlockSpec(memory_space=pl.ANY),
        scratch_shapes=[
            pltpu.VMEM((PAD_BITS[0], HALF), U32),
            pltpu.SemaphoreType.DMA(()),
            pltpu.SemaphoreType.DMA(()),
        ],
    )
    return pl.pallas_call(
        _dispatch_body,
        grid_spec=grid_spec,
        out_shape=jax.ShapeDtypeStruct((N_SLOTS, HALF), U32),
        compiler_params=pltpu.CompilerParams(dimension_semantics=("arbitrary",),
                                             has_side_effects=True),
        name="dispatch",
    )(dest_flat, pad_start, pad_len, hp)


def _unpack_rows(words):
    lo = lax.bitcast_convert_type(words << 16, F32).astype(BF16)
    hi = lax.bitcast_convert_type(words & jnp.uint32(0xFFFF0000), F32).astype(BF16)
    return lo, hi


MOE_PASS_WIDTHS = (4, 2, 1)
assert MOE_TM // MOE_SUB < 2 * MOE_PASS_WIDTHS[0]


def _for_each_pass(nsub, fn):
    for width in MOE_PASS_WIDTHS:
        @pl.when((nsub & width) != 0)
        def _():
            first = (nsub & ~(2 * width - 1)) * MOE_SUB
            fn(pl.multiple_of(first, MOE_SUB), width * MOE_SUB)


def _moe_body(be_ref, bx_ref, ns_ref, nu_ref,
              x_ref, wg_ref, wl_ref, wd_ref, bg_ref, bl_ref, bd_ref,
              o_ref, act_ref):
    b = pl.program_id(0)
    c = pl.program_id(1)
    valid = b < nu_ref[0]
    nsub = ns_ref[b]

    @pl.when(valid & (c < MOE_NCH))
    def _():
        def up(first, n_rows):
            rows = pl.ds(first, n_rows)
            lo, hi = _unpack_rows(x_ref[rows, :])
            g = (jnp.dot(lo, wg_ref[0, :HALF, :].astype(BF16), preferred_element_type=F32)
                 + jnp.dot(hi, wg_ref[0, HALF:, :].astype(BF16), preferred_element_type=F32)
                 + bg_ref[0])
            lin = (jnp.dot(lo, wl_ref[0, :HALF, :].astype(BF16), preferred_element_type=F32)
                   + jnp.dot(hi, wl_ref[0, HALF:, :].astype(BF16), preferred_element_type=F32)
                   + bl_ref[0])
            g = jnp.minimum(g, SWIGLU_LIMIT)
            lin = jnp.clip(lin, -SWIGLU_LIMIT, SWIGLU_LIMIT)
            act = g * jax.nn.sigmoid(SWIGLU_ALPHA * g) * (lin + 1.0)
            act_ref[c, rows, :] = act.astype(BF16)
        _for_each_pass(nsub, up)

    @pl.when(valid & (c >= MOE_NCH))
    def _():
        def down(first, n_rows):
            rows = pl.ds(first, n_rows)
            y = jnp.zeros((n_rows, MOE_CH), F32) + bd_ref[0]
            for k in range(MOE_NCH):
                y = y + jnp.dot(act_ref[k, rows, :],
                                wd_ref[0, k * MOE_CH:(k + 1) * MOE_CH, :].astype(BF16),
                                preferred_element_type=F32)
            o_ref[rows, :] = y
        _for_each_pass(nsub, down)

        def fill(i, carry):
            rows = pl.ds(pl.multiple_of(i * MOE_SUB, MOE_SUB), MOE_SUB)
            o_ref[rows, :] = jnp.zeros((MOE_SUB, MOE_CH), F32)
            return carry
        lax.fori_loop(nsub, MOE_TM // MOE_SUB, fill, 0)


def _moe(blk_expert, blk_x, blk_nsub, n_used, xs, w_gate_up, w_down, b_gate_up, b_down):
    last = MOE_NCH - 1

    def up_chunk(b, c, nu):
        return jnp.where(b < nu[0], jnp.minimum(c, last), last)

    def down_chunk(b, c, nu):
        return jnp.where(b < nu[0], jnp.maximum(c - MOE_NCH, 0), last)

    grid_spec = pltpu.PrefetchScalarGridSpec(
        num_scalar_prefetch=4,
        grid=(n_used[0], 2 * MOE_NCH),
        in_specs=[
            pl.BlockSpec((MOE_TM, HALF), lambda b, c, be, bx, ns, nu: (bx[b], 0)),
            pl.BlockSpec((1, D_MODEL, MOE_CH),
                         lambda b, c, be, bx, ns, nu: (be[b], 0, up_chunk(b, c, nu))),
            pl.BlockSpec((1, D_MODEL, MOE_CH),
                         lambda b, c, be, bx, ns, nu: (be[b], 0, MOE_NCH + up_chunk(b, c, nu))),
            pl.BlockSpec((1, D_FF, MOE_CH),
                         lambda b, c, be, bx, ns, nu: (be[b], 0, down_chunk(b, c, nu))),
            pl.BlockSpec((1, 1, MOE_CH),
                         lambda b, c, be, bx, ns, nu: (be[b], 0, up_chunk(b, c, nu))),
            pl.BlockSpec((1, 1, MOE_CH),
                         lambda b, c, be, bx, ns, nu: (be[b], 0, MOE_NCH + up_chunk(b, c, nu))),
            pl.BlockSpec((1, 1, MOE_CH),
                         lambda b, c, be, bx, ns, nu: (be[b], 0, down_chunk(b, c, nu))),
        ],
        out_specs=pl.BlockSpec((MOE_TM, MOE_CH),
                               lambda b, c, be, bx, ns, nu: (bx[b], down_chunk(b, c, nu))),
        scratch_shapes=[pltpu.VMEM((MOE_NCH, MOE_TM, MOE_CH), BF16)],
    )
    return pl.pallas_call(
        _moe_body,
        grid_spec=grid_spec,
        out_shape=jax.ShapeDtypeStruct((N_SLOTS, D_MODEL), F32),
        compiler_params=_cparams(("arbitrary", "arbitrary"), 60),
        name="moe",
    )(blk_expert, blk_x, blk_nsub, n_used, xs, w_gate_up, w_gate_up, w_down,
      b_gate_up, b_gate_up, b_down)


def _combine_body(dest_ref, gates_ref, h_ref, nw_ref, ys_ref, o_ref, buf_ref, sem):
    i = pl.program_id(0)
    nblk = pl.num_programs(0)

    def row_copy(blk, slot, k, t):
        d = dest_ref[k * SEQ + blk * COMBINE_TB + t]
        return pltpu.make_async_copy(ys_ref.at[pl.ds(d, 1)],
                                     buf_ref.at[slot, k, pl.ds(t, 1)], sem.at[slot])

    def start_block(blk, slot):
        def body(t, carry):
            for k in range(TOP_K):
                row_copy(blk, slot, k, t).start(priority=k % 2)
            return carry
        lax.fori_loop(0, COMBINE_TB, body, 0, unroll=4)

    def wait_block(blk, slot):
        def body(t, carry):
            for k in range(TOP_K):
                row_copy(blk, slot, k, t).wait()
            return carry
        lax.fori_loop(0, COMBINE_TB, body, 0, unroll=4)

    slot = i % 2

    @pl.when(i == 0)
    def _():
        start_block(0, 0)

    @pl.when(i + 1 < nblk)
    def _():
        start_block(i + 1, 1 - slot)

    wait_block(i, slot)

    acc = h_ref[...]
    for k in range(TOP_K):
        acc = acc + gates_ref[:, k:k + 1] * buf_ref[slot, k]
    ms = jnp.mean(acc * acc, axis=-1, keepdims=True)
    o_ref[...] = acc * lax.rsqrt(ms + NORM_EPS) * nw_ref[...]


def _combine(dest_flat, gates_t, h, final_norm_w, ys):
    grid_spec = pltpu.PrefetchScalarGridSpec(
        num_scalar_prefetch=1,
        grid=(SEQ // COMBINE_TB,),
        in_specs=[
            pl.BlockSpec((COMBINE_TB, TOP_K), lambda i, d: (i, 0)),
            pl.BlockSpec((COMBINE_TB, D_MODEL), lambda i, d: (i, 0)),
            pl.BlockSpec((1, D_MODEL), lambda i, d: (0, 0)),
            pl.BlockSpec(memory_space=pl.ANY),
        ],
        out_specs=pl.BlockSpec((COMBINE_TB, D_MODEL), lambda i, d: (i, 0)),
        scratch_shapes=[
            pltpu.VMEM((2, TOP_K, COMBINE_TB, D_MODEL), F32),
            pltpu.SemaphoreType.DMA((2,)),
        ],
    )
    return pl.pallas_call(
        _combine_body,
        grid_spec=grid_spec,
        out_shape=jax.ShapeDtypeStruct((SEQ, D_MODEL), F32),
        compiler_params=_cparams(("arbitrary",), 40),
        name="combine",
    )(dest_flat, gates_t, h, final_norm_w, ys)


def _routing_tables(counts, ids, ranks):
    nblk = (counts + MOE_TM - 1) // MOE_TM
    cum = jnp.cumsum(nblk)
    first_blk = cum - nblk
    n_used = cum[-1]
    first_of = jnp.zeros(ids.shape, I32)
    for e in range(N_EXPERTS):
        first_of = first_of + jnp.where(ids == e, first_blk[e], 0)
    dest = first_of * MOE_TM + ranks
    b = jnp.arange(MOE_NB, dtype=I32)
    valid = b < n_used
    blk_e = jnp.minimum(jnp.searchsorted(cum, b, side="right"), N_EXPERTS - 1).astype(I32)
    blk_e = jnp.where(valid, blk_e, blk_e[n_used - 1])
    rows = jnp.clip(counts[blk_e] - (b - first_blk[blk_e]) * MOE_TM, 0, MOE_TM)
    blk_nsub = jnp.where(valid, (rows + MOE_SUB - 1) // MOE_SUB, 0).astype(I32)
    blk_x = jnp.where(valid, b, n_used - 1).astype(I32)
    pad_start = (first_blk * MOE_TM + counts).astype(I32)
    pad_len = ((-counts) % MOE_SUB).astype(I32)
    return (dest.reshape(-1).astype(I32), blk_e, blk_x, blk_nsub,
            n_used.reshape(1).astype(I32), pad_start, pad_len)


def kernel(x, attn_norm_w, w_in, conv_w, rel_pos_bias, attn_out_norm_w, conv_out_norm_w, w_out,
           ffn_norm_w, router_w, router_b, w_gate_up, b_gate_up, w_down, b_down, final_norm_w):
    bsz, s, d = x.shape
    assert (bsz, s, d) == (1, SEQ, D_MODEL)
    assert attn_norm_w.shape[0] == 1
    x2d = x.reshape(SEQ, D_MODEL)

    proj = _inproj(x2d, attn_norm_w[0].reshape(1, D_MODEL), w_in[0])
    attn_n = _attention(proj, _bias_table(rel_pos_bias[0]), attn_out_norm_w[0].reshape(1, ATTN_WIDTH))
    conv_n = _conv(proj, conv_w[0], conv_out_norm_w[0].reshape(1, CONV_WIDTH))

    router_wt = router_w[0].T.astype(BF16)
    router_b_col = jnp.broadcast_to(router_b[0][:, None], (N_EXPERTS, V7X_LANES))
    h, hp, ids, gates, ranks, cnt = _outproj_router(
        attn_n, conv_n, w_out[0], x2d, ffn_norm_w[0].reshape(1, D_MODEL), router_wt, router_b_col)

    counts = cnt[:, 0].astype(I32)
    dest, blk_e, blk_x, blk_nsub, n_used, pad_start, pad_len = _routing_tables(
        counts, ids[:TOP_K], ranks[:TOP_K])

    xs = _dispatch(dest, pad_start, pad_len, hp)
    ys = _moe(blk_e, blk_x, blk_nsub, n_used, xs, w_gate_up[0], w_down[0],
              b_gate_up[0].reshape(N_EXPERTS, 1, 2 * D_FF), b_down[0].reshape(N_EXPERTS, 1, D_MODEL))
    out = _combine(dest, gates[:TOP_K].T, h, final_norm_w.reshape(1, D_MODEL), ys)
    return out.reshape(bsz, s, d)
```

```python
import functools
import math

import jax
import jax.numpy as jnp
from jax import lax
from jax.experimental import pallas as pl
from jax.experimental.pallas import tpu as pltpu

F32 = jnp.float32
BF16 = jnp.bfloat16
I32 = jnp.int32
U32 = jnp.uint32

D_MODEL = 2048
SEQ = 8192
HEAD_DIM = 128
NA_HEADS = 8
ATTN_WIDTH = NA_HEADS * HEAD_DIM
CONV_WIDTH = D_MODEL - ATTN_WIDTH
CONV_GROUPS = CONV_WIDTH // HEAD_DIM
N_PROJ = 3 * ATTN_WIDTH + 3 * CONV_WIDTH
GRID_W = 64
GRID_ROWS = SEQ // GRID_W
NA_KH = 8
NA_KW = 16
N_EXPERTS = 32
TOP_K = 4
D_FF = D_MODEL
SWIGLU_LIMIT = 7.0
SWIGLU_ALPHA = 1.702
NORM_EPS = 1e-5
NEG_INF = -1e30

V7X_LANES = 128
V7X_VMEM_BYTES = 64 * 1024 * 1024
MIB = 1024 * 1024

INPROJ_TM = 1024
INPROJ_TN = 1024
NORM_ROWS = 256
ATTN_ROWS_PER_STEP = 16
WIN_KEYS = NA_KH * GRID_W
CONV_CHUNK = 1024
OUT_TM = 1024
OUT_TN = 512
OUT_NCH = D_MODEL // OUT_TN
MOE_SUB = 256
MOE_TM = 5 * MOE_SUB
MOE_CH = 512
MOE_NCH = D_FF // MOE_CH
MOE_NB = (SEQ * TOP_K) // MOE_TM + 1 + N_EXPERTS - 1
N_SLOTS = MOE_NB * MOE_TM
HALF = D_MODEL // 2
DISPATCH_TB = 1024
COMBINE_TB = 256


def _cparams(semantics, vmem_mib):
    return pltpu.CompilerParams(dimension_semantics=semantics,
                                vmem_limit_bytes=vmem_mib * MIB)


def _inproj_body(x_ref, nw_ref, w_ref, o_ref, xn_ref):
    @pl.when(pl.program_id(1) == 0)
    def _():
        def chunk(i, carry):
            rows = pl.ds(pl.multiple_of(i * NORM_ROWS, NORM_ROWS), NORM_ROWS)
            x = x_ref[rows, :]
            ms = jnp.mean(x * x, axis=-1, keepdims=True)
            xn_ref[rows, :] = (x * lax.rsqrt(ms + NORM_EPS) * nw_ref[...]).astype(BF16)
            return carry
        lax.fori_loop(0, INPROJ_TM // NORM_ROWS, chunk, 0)

    o_ref[...] = jnp.dot(xn_ref[...], w_ref[...].astype(BF16),
                         preferred_element_type=F32).astype(BF16)


def _inproj(x2d, norm_w, w_in):
    return pl.pallas_call(
        _inproj_body,
        grid=(SEQ // INPROJ_TM, N_PROJ // INPROJ_TN),
        in_specs=[
            pl.BlockSpec((INPROJ_TM, D_MODEL), lambda m, n: (m, 0)),
            pl.BlockSpec((1, D_MODEL), lambda m, n: (0, 0)),
            pl.BlockSpec((D_MODEL, INPROJ_TN), lambda m, n: (0, n)),
        ],
        out_specs=pl.BlockSpec((INPROJ_TM, INPROJ_TN), lambda m, n: (m, n)),
        out_shape=jax.ShapeDtypeStruct((SEQ, N_PROJ), BF16),
        scratch_shapes=[pltpu.VMEM((INPROJ_TM, D_MODEL), BF16)],
        compiler_params=_cparams(("parallel", "arbitrary"), 56),
        name="inproj",
    )(x2d, norm_w, w_in)


def _bias_table(rel_pos_bias):
    cq = jnp.arange(GRID_W)[:, None]
    ck = jnp.arange(GRID_W)[None, :]
    cs = jnp.clip(cq - NA_KW // 2, 0, GRID_W - NA_KW)
    band = (ck >= cs) & (ck < cs + NA_KW)
    dc = jnp.clip(ck - cq, -(NA_KW - 1), NA_KW - 1) + (NA_KW - 1)
    t = jnp.zeros((NA_HEADS, 2 * NA_KH - 1, GRID_W, GRID_W), F32)
    for j in range(2 * NA_KW - 1):
        t = t + jnp.where(dc == j, rel_pos_bias[:, :, j, None, None], 0.0)
    t = jnp.where(band[None, None], t, NEG_INF)
    tb = jnp.stack([t[:, off:off + NA_KH] for off in range(NA_KH)], axis=1)
    return tb.transpose(0, 1, 3, 2, 4).reshape(NA_HEADS, NA_KH, GRID_W, WIN_KEYS).astype(F32)


def _attn_body(q_ref, k_ref, v_ref, tb_ref, nw_ref, o_ref):
    rb = pl.program_id(1)
    scale = 1.0 / math.sqrt(HEAD_DIM)
    for i in range(ATTN_ROWS_PER_STEP):
        r = rb * ATTN_ROWS_PER_STEP + i
        ws = jnp.clip(r - NA_KH // 2, 0, GRID_ROWS - NA_KH)
        off = ws - r + (NA_KH - 1)
        kstart = pl.multiple_of(ws * GRID_W, GRID_W)
        kk = k_ref[pl.ds(kstart, WIN_KEYS), :]
        vv = v_ref[pl.ds(kstart, WIN_KEYS), :]
        q = q_ref[i * GRID_W:(i + 1) * GRID_W, :]
        s = lax.dot_general(q, kk, (((1,), (1,)), ((), ())), preferred_element_type=F32)
        s = s * scale + tb_ref[0, off]
        m = jnp.max(s, axis=-1, keepdims=True)
        p = jnp.exp(s - m)
        l = jnp.sum(p, axis=-1, keepdims=True)
        o = jnp.dot(p.astype(BF16), vv, preferred_element_type=F32) / l
        ms = jnp.mean(o * o, axis=-1, keepdims=True)
        o_ref[i * GRID_W:(i + 1) * GRID_W, :] = (
            o * lax.rsqrt(ms + NORM_EPS) * nw_ref[...]).astype(BF16)


def _attention(proj, bias_tab, attn_out_norm_w):
    rows = ATTN_ROWS_PER_STEP * GRID_W
    return pl.pallas_call(
        _attn_body,
        grid=(NA_HEADS, GRID_ROWS // ATTN_ROWS_PER_STEP),
        in_specs=[
            pl.BlockSpec((rows, HEAD_DIM), lambda h, rb: (rb, h)),
            pl.BlockSpec((SEQ, HEAD_DIM), lambda h, rb: (0, NA_HEADS + h)),
            pl.BlockSpec((SEQ, HEAD_DIM), lambda h, rb: (0, 2 * NA_HEADS + h)),
            pl.BlockSpec((1, NA_KH, GRID_W, WIN_KEYS), lambda h, rb: (h, 0, 0, 0)),
            pl.BlockSpec((1, HEAD_DIM), lambda h, rb: (0, h)),
        ],
        out_specs=pl.BlockSpec((rows, HEAD_DIM), lambda h, rb: (rb, h)),
        out_shape=jax.ShapeDtypeStruct((SEQ, ATTN_WIDTH), BF16),
        compiler_params=_cparams(("parallel", "arbitrary"), 32),
        name="attn",
    )(proj, proj, proj, bias_tab, attn_out_norm_w)


CONV_PAD = 8


def _conv_body(u_ref, b_ref, c_ref, cw_ref, nw_ref, o_ref, z_ref):
    zeros = jnp.zeros((CONV_PAD, HEAD_DIM), F32)
    z_ref[0:CONV_PAD, :] = zeros
    z_ref[CONV_PAD + SEQ:2 * CONV_PAD + SEQ, :] = zeros
    for ch in range(SEQ // CONV_CHUNK):
        lo = ch * CONV_CHUNK
        z_ref[CONV_PAD + lo:CONV_PAD + lo + CONV_CHUNK, :] = (
            c_ref[lo:lo + CONV_CHUNK, :].astype(F32) * u_ref[lo:lo + CONV_CHUNK, :].astype(F32))
    w0 = cw_ref[0:1, :]
    w1 = cw_ref[1:2, :]
    w2 = cw_ref[2:3, :]
    for ch in range(SEQ // CONV_CHUNK):
        lo = ch * CONV_CHUNK
        base = CONV_PAD + lo
        conv = (w0 * z_ref[base - 1:base - 1 + CONV_CHUNK, :]
                + w1 * z_ref[base:base + CONV_CHUNK, :]
                + w2 * z_ref[base + 1:base + 1 + CONV_CHUNK, :])
        y = b_ref[lo:lo + CONV_CHUNK, :].astype(F32) * conv
        ms = jnp.mean(y * y, axis=-1, keepdims=True)
        o_ref[lo:lo + CONV_CHUNK, :] = (y * lax.rsqrt(ms + NORM_EPS) * nw_ref[...]).astype(BF16)


def _conv(proj, conv_w, conv_out_norm_w):
    col0 = 3 * NA_HEADS
    return pl.pallas_call(
        _conv_body,
        grid=(CONV_GROUPS,),
        in_specs=[
            pl.BlockSpec((SEQ, HEAD_DIM), lambda g: (0, col0 + g)),
            pl.BlockSpec((SEQ, HEAD_DIM), lambda g: (0, col0 + CONV_GROUPS + g)),
            pl.BlockSpec((SEQ, HEAD_DIM), lambda g: (0, col0 + 2 * CONV_GROUPS + g)),
            pl.BlockSpec((3, HEAD_DIM), lambda g: (0, g)),
            pl.BlockSpec((1, HEAD_DIM), lambda g: (0, g)),
        ],
        out_specs=pl.BlockSpec((SEQ, HEAD_DIM), lambda g: (0, g)),
        out_shape=jax.ShapeDtypeStruct((SEQ, CONV_WIDTH), BF16),
        scratch_shapes=[pltpu.VMEM((SEQ + 2 * CONV_PAD, HEAD_DIM), F32)],
        compiler_params=_cparams(("arbitrary",), 48),
        name="conv",
    )(proj, proj, proj, conv_w, conv_out_norm_w)


def _bf16_bits(x):
    return lax.bitcast_convert_type(x.astype(BF16).astype(F32), U32) & jnp.uint32(0xFFFF0000)


def _outproj_body(a_ref, c_ref, w_ref, x_ref, fw_ref, rwt_ref, rb_ref,
                  h_ref, hp_ref, ids_ref, gates_ref, ranks_ref, cnt_ref,
                  hacc_ref, carry_ref, tri_ref):
    m = pl.program_id(0)
    n = pl.program_id(1)

    @pl.when((m == 0) & (n == 0))
    def _():
        carry_ref[...] = jnp.zeros_like(carry_ref)
        r = lax.broadcasted_iota(I32, (OUT_TM, OUT_TM), 0)
        c = lax.broadcasted_iota(I32, (OUT_TM, OUT_TM), 1)
        tri_ref[...] = (r < c).astype(BF16)

    w = w_ref[...].astype(BF16)
    acc = (jnp.dot(a_ref[...], w[:ATTN_WIDTH], preferred_element_type=F32)
           + jnp.dot(c_ref[...], w[ATTN_WIDTH:], preferred_element_type=F32)
           + x_ref[...])
    h_ref[...] = acc
    hacc_ref[n] = acc

    @pl.when(n == OUT_NCH - 1)
    def _():
        ss = jnp.zeros((OUT_TM, 1), F32)
        for j in range(OUT_NCH):
            hj = hacc_ref[j]
            ss = ss + jnp.sum(hj * hj, axis=-1, keepdims=True)
        inv = lax.rsqrt(ss * (1.0 / D_MODEL) + NORM_EPS)
        hn = []
        logits = jnp.zeros((N_EXPERTS, OUT_TM), F32)
        for j in range(OUT_NCH):
            hj = (hacc_ref[j] * inv * fw_ref[:, j * OUT_TN:(j + 1) * OUT_TN]).astype(BF16)
            hn.append(hj)
            logits = logits + lax.dot_general(
                rwt_ref[:, j * OUT_TN:(j + 1) * OUT_TN], hj,
                (((1,), (1,)), ((), ())), preferred_element_type=F32)
        logits = logits + rb_ref[:, 0:1]
        half_chunks = OUT_NCH // 2
        for j in range(half_chunks):
            lo = _bf16_bits(hn[j].astype(F32)) >> 16
            hi = _bf16_bits(hn[j + half_chunks].astype(F32))
            hp_ref[:, j * OUT_TN:(j + 1) * OUT_TN] = lo | hi

        eio = lax.broadcasted_iota(I32, (N_EXPERTS, OUT_TM), 0)
        work = logits
        tops, idxs, hots = [], [], []
        for _ in range(TOP_K):
            mx = jnp.max(work, axis=0, keepdims=True)
            idx = jnp.min(jnp.where(work == mx, eio, N_EXPERTS), axis=0, keepdims=True)
            hot = eio == idx
            tops.append(mx)
            idxs.append(idx)
            hots.append(hot)
            work = jnp.where(hot, -jnp.inf, work)
        exps = [jnp.exp(t - tops[0]) for t in tops]
        den = exps[0] + exps[1] + exps[2] + exps[3]
        multi = jnp.zeros((N_EXPERTS, OUT_TM), F32)
        for hot in hots:
            multi = multi + hot.astype(F32)
        carry = carry_ref[:, 0:1]
        before = carry + jnp.dot(multi.astype(BF16), tri_ref[...], preferred_element_type=F32)
        zero_i = jnp.zeros((1, OUT_TM), I32)
        zero_f = jnp.zeros((1, OUT_TM), F32)
        for k in range(TOP_K):
            ids_ref[k:k + 1, :] = idxs[k]
            gates_ref[k:k + 1, :] = exps[k] / den
            rank = jnp.sum(jnp.where(hots[k], before, 0.0), axis=0, keepdims=True)
            ranks_ref[k:k + 1, :] = rank.astype(I32)
            ids_ref[TOP_K + k:TOP_K + k + 1, :] = zero_i
            gates_ref[TOP_K + k:TOP_K + k + 1, :] = zero_f
            ranks_ref[TOP_K + k:TOP_K + k + 1, :] = zero_i
        new_carry = carry + jnp.sum(multi, axis=1, keepdims=True)
        full = jnp.broadcast_to(new_carry, (N_EXPERTS, V7X_LANES))
        carry_ref[...] = full
        cnt_ref[...] = full


def _outproj_router(attn_n, conv_n, w_out, x2d, ffn_norm_w, router_wt, router_b_col):
    grid = (SEQ // OUT_TM, OUT_NCH)
    meta = lambda m, n: (0, m)
    return pl.pallas_call(
        _outproj_body,
        grid=grid,
        in_specs=[
            pl.BlockSpec((OUT_TM, ATTN_WIDTH), lambda m, n: (m, 0)),
            pl.BlockSpec((OUT_TM, CONV_WIDTH), lambda m, n: (m, 0)),
            pl.BlockSpec((D_MODEL, OUT_TN), lambda m, n: (0, n)),
            pl.BlockSpec((OUT_TM, OUT_TN), lambda m, n: (m, n)),
            pl.BlockSpec((1, D_MODEL), lambda m, n: (0, 0)),
            pl.BlockSpec((N_EXPERTS, D_MODEL), lambda m, n: (0, 0)),
            pl.BlockSpec((N_EXPERTS, V7X_LANES), lambda m, n: (0, 0)),
        ],
        out_specs=[
            pl.BlockSpec((OUT_TM, OUT_TN), lambda m, n: (m, n)),
            pl.BlockSpec((OUT_TM, HALF), lambda m, n: (m, 0)),
            pl.BlockSpec((2 * TOP_K, OUT_TM), meta),
            pl.BlockSpec((2 * TOP_K, OUT_TM), meta),
            pl.BlockSpec((2 * TOP_K, OUT_TM), meta),
            pl.BlockSpec((N_EXPERTS, V7X_LANES), lambda m, n: (0, 0)),
        ],
        out_shape=[
            jax.ShapeDtypeStruct((SEQ, D_MODEL), F32),
            jax.ShapeDtypeStruct((SEQ, HALF), U32),
            jax.ShapeDtypeStruct((2 * TOP_K, SEQ), I32),
            jax.ShapeDtypeStruct((2 * TOP_K, SEQ), F32),
            jax.ShapeDtypeStruct((2 * TOP_K, SEQ), I32),
            jax.ShapeDtypeStruct((N_EXPERTS, V7X_LANES), F32),
        ],
        scratch_shapes=[
            pltpu.VMEM((OUT_NCH, OUT_TM, OUT_TN), F32),
            pltpu.VMEM((N_EXPERTS, V7X_LANES), F32),
            pltpu.VMEM((OUT_TM, OUT_TM), BF16),
        ],
        compiler_params=_cparams(("arbitrary", "arbitrary"), 56),
        name="outproj_router",
    )(attn_n, conv_n, w_out, x2d, ffn_norm_w, router_wt, router_b_col)


HBM_ROW_TILE = 8
PAD_BITS = (128, 64, 32, 16, 8)


def _dispatch_body(dest_ref, pad_start_ref, pad_len_ref, hp_ref, xs_ref, zbuf_ref, sem, zsem):
    i = pl.program_id(0)

    def row_copy(t, d):
        return pltpu.make_async_copy(hp_ref.at[pl.ds(t, 1)], xs_ref.at[pl.ds(d, 1)], sem)

    def pad_head(e):
        return (-pad_start_ref[e]) & (HBM_ROW_TILE - 1)

    def pad_row_copy(e, j):
        return pltpu.make_async_copy(zbuf_ref.at[pl.ds(0, 1)],
                                     xs_ref.at[pl.ds(pad_start_ref[e] + j, 1)], zsem)

    def pad_piece_copy(e, bit):
        body = pad_len_ref[e] - pad_head(e)
        start = pad_start_ref[e] + pad_head(e) + (body & ~(2 * bit - 1))
        start = pl.multiple_of(start, HBM_ROW_TILE)
        return pltpu.make_async_copy(zbuf_ref.at[pl.ds(0, bit)], xs_ref.at[pl.ds(start, bit)], zsem)

    def for_each_pad(action):
        def per_expert(e, carry):
            for j in range(HBM_ROW_TILE - 1):
                @pl.when(j < pad_head(e))
                def _():
                    action(pad_row_copy(e, j))
            for bit in PAD_BITS:
                @pl.when(((pad_len_ref[e] - pad_head(e)) & bit) != 0)
                def _():
                    action(pad_piece_copy(e, bit))
            return carry
        lax.fori_loop(0, N_EXPERTS, per_expert, 0)

    @pl.when(i == 0)
    def _():
        zbuf_ref[...] = jnp.zeros_like(zbuf_ref)
        for_each_pad(lambda cp: cp.start())
        for_each_pad(lambda cp: cp.wait())

    def start_rows(t, carry):
        tok = i * DISPATCH_TB + t
        for k in range(TOP_K):
            row_copy(t, dest_ref[k * SEQ + tok]).start(priority=k % 2)
        return carry
    lax.fori_loop(0, DISPATCH_TB, start_rows, 0, unroll=8)

    def wait_rows(t, carry):
        tok = i * DISPATCH_TB + t
        for k in range(TOP_K):
            row_copy(t, dest_ref[k * SEQ + tok]).wait()
        return carry
    lax.fori_loop(0, DISPATCH_TB, wait_rows, 0, unroll=8)


def _dispatch(dest_flat, pad_start, pad_len, hp):
    grid_spec = pltpu.PrefetchScalarGridSpec(
        num_scalar_prefetch=3,
        grid=(SEQ // DISPATCH_TB,),
        in_specs=[pl.BlockSpec((DISPATCH_TB, HALF), lambda i, d, ps, pn: (i, 0))],
        out_specs=pl.BlockSpec(memory_space=pl.ANY),
        scratch_shapes=[
            pltpu.VMEM((PAD_BITS[0], HALF), U32),
            pltpu.SemaphoreType.DMA(()),
            pltpu.SemaphoreType.DMA(()),
        ],
    )
    return pl.pallas_call(
        _dispatch_body,
        grid_spec=grid_spec,
        out_shape=jax.ShapeDtypeStruct((N_SLOTS, HALF), U32),
        compiler_params=pltpu.CompilerParams(dimension_semantics=("arbitrary",),
                                             has_side_effects=True),
        name="dispatch",
    )(dest_flat, pad_start, pad_len, hp)


def _unpack_rows(words):
    lo = lax.bitcast_convert_type(words << 16, F32).astype(BF16)
    hi = lax.bitcast_convert_type(words & jnp.uint32(0xFFFF0000), F32).astype(BF16)
    return lo, hi


MOE_PASS_WIDTHS = (4, 2, 1)
assert MOE_TM // MOE_SUB < 2 * MOE_PASS_WIDTHS[0]


def _for_each_pass(nsub, fn):
    for width in MOE_PASS_WIDTHS:
        @pl.when((nsub & width) != 0)
        def _():
            first = (nsub & ~(2 * width - 1)) * MOE_SUB
            fn(pl.multiple_of(first, MOE_SUB), width * MOE_SUB)


def _moe_body(be_ref, bx_ref, ns_ref, nu_ref,
              x_ref, wg_ref, wl_ref, wd_ref, bg_ref, bl_ref, bd_ref,
              o_ref, act_ref):
    b = pl.program_id(0)
    c = pl.program_id(1)
    valid = b < nu_ref[0]
    nsub = ns_ref[b]

    @pl.when(valid & (c < MOE_NCH))
    def _():
        def up(first, n_rows):
            rows = pl.ds(first, n_rows)
            lo, hi = _unpack_rows(x_ref[rows, :])
            g = (jnp.dot(lo, wg_ref[0, :HALF, :].astype(BF16), preferred_element_type=F32)
                 + jnp.dot(hi, wg_ref[0, HALF:, :].astype(BF16), preferred_element_type=F32)
                 + bg_ref[0])
            lin = (jnp.dot(lo, wl_ref[0, :HALF, :].astype(BF16), preferred_element_type=F32)
                   + jnp.dot(hi, wl_ref[0, HALF:, :].astype(BF16), preferred_element_type=F32)
                   + bl_ref[0])
            g = jnp.minimum(g, SWIGLU_LIMIT)
            lin = jnp.clip(lin, -SWIGLU_LIMIT, SWIGLU_LIMIT)
            act = g * jax.nn.sigmoid(SWIGLU_ALPHA * g) * (lin + 1.0)
            act_ref[c, rows, :] = act.astype(BF16)
        _for_each_pass(nsub, up)

    @pl.when(valid & (c >= MOE_NCH))
    def _():
        def down(first, n_rows):
            rows = pl.ds(first, n_rows)
            y = jnp.zeros((n_rows, MOE_CH), F32) + bd_ref[0]
            for k in range(MOE_NCH):
                y = y + jnp.dot(act_ref[k, rows, :],
                                wd_ref[0, k * MOE_CH:(k + 1) * MOE_CH, :].astype(BF16),
                                preferred_element_type=F32)
            o_ref[rows, :] = y
        _for_each_pass(nsub, down)

        def fill(i, carry):
            rows = pl.ds(pl.multiple_of(i * MOE_SUB, MOE_SUB), MOE_SUB)
            o_ref[rows, :] = jnp.zeros((MOE_SUB, MOE_CH), F32)
            return carry
        lax.fori_loop(nsub, MOE_TM // MOE_SUB, fill, 0)


def _moe(blk_expert, blk_x, blk_nsub, n_used, xs, w_gate_up, w_down, b_gate_up, b_down):
    last = MOE_NCH - 1

    def up_chunk(b, c, nu):
        return jnp.where(b < nu[0], jnp.minimum(c, last), last)

    def down_chunk(b, c, nu):
        return jnp.where(b < nu[0], jnp.maximum(c - MOE_NCH, 0), last)

    grid_spec = pltpu.PrefetchScalarGridSpec(
        num_scalar_prefetch=4,
        grid=(n_used[0], 2 * MOE_NCH),
        in_specs=[
            pl.BlockSpec((MOE_TM, HALF), lambda b, c, be, bx, ns, nu: (bx[b], 0)),
            pl.BlockSpec((1, D_MODEL, MOE_CH),
                         lambda b, c, be, bx, ns, nu: (be[b], 0, up_chunk(b, c, nu))),
            pl.BlockSpec((1, D_MODEL, MOE_CH),
                         lambda b, c, be, bx, ns, nu: (be[b], 0, MOE_NCH + up_chunk(b, c, nu))),
            pl.BlockSpec((1, D_FF, MOE_CH),
                         lambda b, c, be, bx, ns, nu: (be[b], 0, down_chunk(b, c, nu))),
            pl.BlockSpec((1, 1, MOE_CH),
                         lambda b, c, be, bx, ns, nu: (be[b], 0, up_chunk(b, c, nu))),
            pl.BlockSpec((1, 1, MOE_CH),
                         lambda b, c, be, bx, ns, nu: (be[b], 0, MOE_NCH + up_chunk(b, c, nu))),
            pl.BlockSpec((1, 1, MOE_CH),
                         lambda b, c, be, bx, ns, nu: (be[b], 0, down_chunk(b, c, nu))),
        ],
        out_specs=pl.BlockSpec((MOE_TM, MOE_CH),
                               lambda b, c, be, bx, ns, nu: (bx[b], down_chunk(b, c, nu))),
        scratch_shapes=[pltpu.VMEM((MOE_NCH, MOE_TM, MOE_CH), BF16)],
    )
    return pl.pallas_call(
        _moe_body,
        grid_spec=grid_spec,
        out_shape=jax.ShapeDtypeStruct((N_SLOTS, D_MODEL), F32),
        compiler_params=_cparams(("arbitrary", "arbitrary"), 60),
        name="moe",
    )(blk_expert, blk_x, blk_nsub, n_used, xs, w_gate_up, w_gate_up, w_down,
      b_gate_up, b_gate_up, b_down)


def _combine_body(dest_ref, gates_ref, h_ref, nw_ref, ys_ref, o_ref, buf_ref, sem):
    i = pl.program_id(0)
    nblk = pl.num_programs(0)

    def row_copy(blk, slot, k, t):
        d = dest_ref[k * SEQ + blk * COMBINE_TB + t]
        return pltpu.make_async_copy(ys_ref.at[pl.ds(d, 1)],
                                     buf_ref.at[slot, k, pl.ds(t, 1)], sem.at[slot])

    def start_block(blk, slot):
        def body(t, carry):
            for k in range(TOP_K):
                row_copy(blk, slot, k, t).start(priority=k % 2)
            return carry
        lax.fori_loop(0, COMBINE_TB, body, 0, unroll=8)

    def wait_block(blk, slot):
        def body(t, carry):
            for k in range(TOP_K):
                row_copy(blk, slot, k, t).wait()
            return carry
        lax.fori_loop(0, COMBINE_TB, body, 0, unroll=8)

    slot = i % 2

    @pl.when(i == 0)
    def _():
        start_block(0, 0)

    @pl.when(i + 1 < nblk)
    def _():
        start_block(i + 1, 1 - slot)

    wait_block(i, slot)

    acc = h_ref[...]
    for k in range(TOP_K):
        acc = acc + gates_ref[:, k:k + 1] * buf_ref[slot, k]
    ms = jnp.mean(acc * acc, axis=-1, keepdims=True)
    o_ref[...] = acc * lax.rsqrt(ms + NORM_EPS) * nw_ref[...]


def _combine(dest_flat, gates_t, h, final_norm_w, ys):
    grid_spec = pltpu.PrefetchScalarGridSpec(
        num_scalar_prefetch=1,
        grid=(SEQ // COMBINE_TB,),
        in_specs=[
            pl.BlockSpec((COMBINE_TB, TOP_K), lambda i, d: (i, 0)),
            pl.BlockSpec((COMBINE_TB, D_MODEL), lambda i, d: (i, 0)),
            pl.BlockSpec((1, D_MODEL), lambda i, d: (0, 0)),
            pl.BlockSpec(memory_space=pl.ANY),
        ],
        out_specs=pl.BlockSpec((COMBINE_TB, D_MODEL), lambda i, d: (i, 0)),
        scratch_shapes=[
            pltpu.VMEM((2, TOP_K, COMBINE_TB, D_MODEL), F32),
            pltpu.SemaphoreType.DMA((2,)),
        ],
    )
    return pl.pallas_call(
        _combine_body,
        grid_spec=grid_spec,
        out_shape=jax.ShapeDtypeStruct((SEQ, D_MODEL), F32),
        compiler_params=_cparams(("arbitrary",), 40),
        name="combine",
    )(dest_flat, gates_t, h, final_norm_w, ys)


def _routing_tables(counts, ids, ranks):
    nblk = (counts + MOE_TM - 1) // MOE_TM
    cum = jnp.cumsum(nblk)
    first_blk = cum - nblk
    n_used = cum[-1]
    first_of = jnp.zeros(ids.shape, I32)
    for e in range(N_EXPERTS):
        first_of = first_of + jnp.where(ids == e, first_blk[e], 0)
    dest = first_of * MOE_TM + ranks
    b = jnp.arange(MOE_NB, dtype=I32)
    valid = b < n_used
    blk_e = jnp.minimum(jnp.searchsorted(cum, b, side="right"), N_EXPERTS - 1).astype(I32)
    blk_e = jnp.where(valid, blk_e, blk_e[n_used - 1])
    rows = jnp.clip(counts[blk_e] - (b - first_blk[blk_e]) * MOE_TM, 0, MOE_TM)
    blk_nsub = jnp.where(valid, (rows + MOE_SUB - 1) // MOE_SUB, 0).astype(I32)
    blk_x = jnp.where(valid, b, n_used - 1).astype(I32)
    pad_start = (first_blk * MOE_TM + counts).astype(I32)
    pad_len = ((-counts) % MOE_SUB).astype(I32)
    return (dest.reshape(-1).astype(I32), blk_e, blk_x, blk_nsub,
            n_used.reshape(1).astype(I32), pad_start, pad_len)


def kernel(x, attn_norm_w, w_in, conv_w, rel_pos_bias, attn_out_norm_w, conv_out_norm_w, w_out,
           ffn_norm_w, router_w, router_b, w_gate_up, b_gate_up, w_down, b_down, final_norm_w):
    bsz, s, d = x.shape
    assert (bsz, s, d) == (1, SEQ, D_MODEL)
    assert attn_norm_w.shape[0] == 1
    x2d = x.reshape(SEQ, D_MODEL)

    proj = _inproj(x2d, attn_norm_w[0].reshape(1, D_MODEL), w_in[0])
    attn_n = _attention(proj, _bias_table(rel_pos_bias[0]), attn_out_norm_w[0].reshape(1, ATTN_WIDTH))
    conv_n = _conv(proj, conv_w[0], conv_out_norm_w[0].reshape(1, CONV_WIDTH))

    router_wt = router_w[0].T.astype(BF16)
    router_b_col = jnp.broadcast_to(router_b[0][:, None], (N_EXPERTS, V7X_LANES))
    h, hp, ids, gates, ranks, cnt = _outproj_router(
        attn_n, conv_n, w_out[0], x2d, ffn_norm_w[0].reshape(1, D_MODEL), router_wt, router_b_col)

    counts = cnt[:, 0].astype(I32)
    dest, blk_e, blk_x, blk_nsub, n_used, pad_start, pad_len = _routing_tables(
        counts, ids[:TOP_K], ranks[:TOP_K])

    xs = _dispatch(dest, pad_start, pad_len, hp)
    ys = _moe(blk_e, blk_x, blk_nsub, n_used, xs, w_gate_up[0], w_down[0],
              b_gate_up[0].reshape(N_EXPERTS, 1, 2 * D_FF), b_down[0].reshape(N_EXPERTS, 1, D_MODEL))
    out = _combine(dest, gates[:TOP_K].T, h, final_norm_w.reshape(1, D_MODEL), ys)
    return out.reshape(bsz, s, d)
```

```python
import functools
import math

import jax
import jax.numpy as jnp
from jax import lax
from jax.experimental import pallas as pl
from jax.experimental.pallas import tpu as pltpu

F32 = jnp.float32
BF16 = jnp.bfloat16
I32 = jnp.int32
U32 = jnp.uint32

D_MODEL = 2048
SEQ = 8192
HEAD_DIM = 128
NA_HEADS = 8
ATTN_WIDTH = NA_HEADS * HEAD_DIM
CONV_WIDTH = D_MODEL - ATTN_WIDTH
CONV_GROUPS = CONV_WIDTH // HEAD_DIM
N_PROJ = 3 * ATTN_WIDTH + 3 * CONV_WIDTH
GRID_W = 64
GRID_ROWS = SEQ // GRID_W
NA_KH = 8
NA_KW = 16
N_EXPERTS = 32
TOP_K = 4
D_FF = D_MODEL
SWIGLU_LIMIT = 7.0
SWIGLU_ALPHA = 1.702
NORM_EPS = 1e-5
NEG_INF = -1e30

V7X_LANES = 128
V7X_VMEM_BYTES = 64 * 1024 * 1024
MIB = 1024 * 1024

INPROJ_TM = 1024
INPROJ_TN = 1024
NORM_ROWS = 256
ATTN_ROWS_PER_STEP = 16
WIN_KEYS = NA_KH * GRID_W
CONV_CHUNK = 1024
OUT_TM = 1024
OUT_TN = 512
OUT_NCH = D_MODEL // OUT_TN
MOE_SUB = 256
MOE_TM = 5 * MOE_SUB
MOE_CH = 512
MOE_NCH = D_FF // MOE_CH
MOE_NB = (SEQ * TOP_K) // MOE_TM + 1 + N_EXPERTS - 1
N_SLOTS = MOE_NB * MOE_TM
HALF = D_MODEL // 2
DISPATCH_TB = 1024
COMBINE_TB = 256


def _cparams(semantics, vmem_mib):
    return pltpu.CompilerParams(dimension_semantics=semantics,
                                vmem_limit_bytes=vmem_mib * MIB)


def _inproj_body(x_ref, nw_ref, w_ref, o_ref, xn_ref):
    @pl.when(pl.program_id(1) == 0)
    def _():
        def chunk(i, carry):
            rows = pl.ds(pl.multiple_of(i * NORM_ROWS, NORM_ROWS), NORM_ROWS)
            x = x_ref[rows, :]
            ms = jnp.mean(x * x, axis=-1, keepdims=True)
            xn_ref[rows, :] = (x * lax.rsqrt(ms + NORM_EPS) * nw_ref[...]).astype(BF16)
            return carry
        lax.fori_loop(0, INPROJ_TM // NORM_ROWS, chunk, 0)

    o_ref[...] = jnp.dot(xn_ref[...], w_ref[...].astype(BF16),
                         preferred_element_type=F32).astype(BF16)


def _inproj(x2d, norm_w, w_in):
    return pl.pallas_call(
        _inproj_body,
        grid=(SEQ // INPROJ_TM, N_PROJ // INPROJ_TN),
        in_specs=[
            pl.BlockSpec((INPROJ_TM, D_MODEL), lambda m, n: (m, 0)),
            pl.BlockSpec((1, D_MODEL), lambda m, n: (0, 0)),
            pl.BlockSpec((D_MODEL, INPROJ_TN), lambda m, n: (0, n)),
        ],
        out_specs=pl.BlockSpec((INPROJ_TM, INPROJ_TN), lambda m, n: (m, n)),
        out_shape=jax.ShapeDtypeStruct((SEQ, N_PROJ), BF16),
        scratch_shapes=[pltpu.VMEM((INPROJ_TM, D_MODEL), BF16)],
        compiler_params=_cparams(("parallel", "arbitrary"), 56),
        name="inproj",
    )(x2d, norm_w, w_in)


def _bias_table(rel_pos_bias):
    cq = jnp.arange(GRID_W)[:, None]
    ck = jnp.arange(GRID_W)[None, :]
    cs = jnp.clip(cq - NA_KW // 2, 0, GRID_W - NA_KW)
    band = (ck >= cs) & (ck < cs + NA_KW)
    dc = jnp.clip(ck - cq, -(NA_KW - 1), NA_KW - 1) + (NA_KW - 1)
    t = jnp.zeros((NA_HEADS, 2 * NA_KH - 1, GRID_W, GRID_W), F32)
    for j in range(2 * NA_KW - 1):
        t = t + jnp.where(dc == j, rel_pos_bias[:, :, j, None, None], 0.0)
    t = jnp.where(band[None, None], t, NEG_INF)
    tb = jnp.stack([t[:, off:off + NA_KH] for off in range(NA_KH)], axis=1)
    return tb.transpose(0, 1, 3, 2, 4).reshape(NA_HEADS, NA_KH, GRID_W, WIN_KEYS).astype(F32)


def _attn_body(q_ref, k_ref, v_ref, tb_ref, nw_ref, o_ref):
    rb = pl.program_id(1)
    scale = 1.0 / math.sqrt(HEAD_DIM)
    for i in range(ATTN_ROWS_PER_STEP):
        r = rb * ATTN_ROWS_PER_STEP + i
        ws = jnp.clip(r - NA_KH // 2, 0, GRID_ROWS - NA_KH)
        off = ws - r + (NA_KH - 1)
        kstart = pl.multiple_of(ws * GRID_W, GRID_W)
        kk = k_ref[pl.ds(kstart, WIN_KEYS), :]
        vv = v_ref[pl.ds(kstart, WIN_KEYS), :]
        q = q_ref[i * GRID_W:(i + 1) * GRID_W, :]
        s = lax.dot_general(q, kk, (((1,), (1,)), ((), ())), preferred_element_type=F32)
        s = s * scale + tb_ref[0, off]
        m = jnp.max(s, axis=-1, keepdims=True)
        p = jnp.exp(s - m)
        l = jnp.sum(p, axis=-1, keepdims=True)
        o = jnp.dot(p.astype(BF16), vv, preferred_element_type=F32) / l
        ms = jnp.mean(o * o, axis=-1, keepdims=True)
        o_ref[i * GRID_W:(i + 1) * GRID_W, :] = (
            o * lax.rsqrt(ms + NORM_EPS) * nw_ref[...]).astype(BF16)


def _attention(proj, bias_tab, attn_out_norm_w):
    rows = ATTN_ROWS_PER_STEP * GRID_W
    return pl.pallas_call(
        _attn_body,
        grid=(NA_HEADS, GRID_ROWS // ATTN_ROWS_PER_STEP),
        in_specs=[
            pl.BlockSpec((rows, HEAD_DIM), lambda h, rb: (rb, h)),
            pl.BlockSpec((SEQ, HEAD_DIM), lambda h, rb: (0, NA_HEADS + h)),
            pl.BlockSpec((SEQ, HEAD_DIM), lambda h, rb: (0, 2 * NA_HEADS + h)),
            pl.BlockSpec((1, NA_KH, GRID_W, WIN_KEYS), lambda h, rb: (h, 0, 0, 0)),
            pl.BlockSpec((1, HEAD_DIM), lambda h, rb: (0, h)),
        ],
        out_specs=pl.BlockSpec((rows, HEAD_DIM), lambda h, rb: (rb, h)),
        out_shape=jax.ShapeDtypeStruct((SEQ, ATTN_WIDTH), BF16),
        compiler_params=_cparams(("parallel", "arbitrary"), 32),
        name="attn",
    )(proj, proj, proj, bias_tab, attn_out_norm_w)


CONV_PAD = 8


def _conv_body(u_ref, b_ref, c_ref, cw_ref, nw_ref, o_ref, z_ref):
    zeros = jnp.zeros((CONV_PAD, HEAD_DIM), F32)
    z_ref[0:CONV_PAD, :] = zeros
    z_ref[CONV_PAD + SEQ:2 * CONV_PAD + SEQ, :] = zeros
    for ch in range(SEQ // CONV_CHUNK):
        lo = ch * CONV_CHUNK
        z_ref[CONV_PAD + lo:CONV_PAD + lo + CONV_CHUNK, :] = (
            c_ref[lo:lo + CONV_CHUNK, :].astype(F32) * u_ref[lo:lo + CONV_CHUNK, :].astype(F32))
    w0 = cw_ref[0:1, :]
    w1 = cw_ref[1:2, :]
    w2 = cw_ref[2:3, :]
    for ch in range(SEQ // CONV_CHUNK):
        lo = ch * CONV_CHUNK
        base = CONV_PAD + lo
        conv = (w0 * z_ref[base - 1:base - 1 + CONV_CHUNK, :]
                + w1 * z_ref[base:base + CONV_CHUNK, :]
                + w2 * z_ref[base + 1:base + 1 + CONV_CHUNK, :])
        y = b_ref[lo:lo + CONV_CHUNK, :].astype(F32) * conv
        ms = jnp.mean(y * y, axis=-1, keepdims=True)
        o_ref[lo:lo + CONV_CHUNK, :] = (y * lax.rsqrt(ms + NORM_EPS) * nw_ref[...]).astype(BF16)


def _conv(proj, conv_w, conv_out_norm_w):
    col0 = 3 * NA_HEADS
    return pl.pallas_call(
        _conv_body,
        grid=(CONV_GROUPS,),
        in_specs=[
            pl.BlockSpec((SEQ, HEAD_DIM), lambda g: (0, col0 + g)),
            pl.BlockSpec((SEQ, HEAD_DIM), lambda g: (0, col0 + CONV_GROUPS + g)),
            pl.BlockSpec((SEQ, HEAD_DIM), lambda g: (0, col0 + 2 * CONV_GROUPS + g)),
            pl.BlockSpec((3, HEAD_DIM), lambda g: (0, g)),
            pl.BlockSpec((1, HEAD_DIM), lambda g: (0, g)),
        ],
        out_specs=pl.BlockSpec((SEQ, HEAD_DIM), lambda g: (0, g)),
        out_shape=jax.ShapeDtypeStruct((SEQ, CONV_WIDTH), BF16),
        scratch_shapes=[pltpu.VMEM((SEQ + 2 * CONV_PAD, HEAD_DIM), F32)],
        compiler_params=_cparams(("arbitrary",), 48),
        name="conv",
    )(proj, proj, proj, conv_w, conv_out_norm_w)


def _bf16_bits(x):
    return lax.bitcast_convert_type(x.astype(BF16).astype(F32), U32) & jnp.uint32(0xFFFF0000)


def _outproj_body(a_ref, c_ref, w_ref, x_ref, fw_ref, rwt_ref, rb_ref,
                  h_ref, hp_ref, ids_ref, gates_ref, ranks_ref, cnt_ref,
                  hacc_ref, carry_ref, tri_ref):
    m = pl.program_id(0)
    n = pl.program_id(1)

    @pl.when((m == 0) & (n == 0))
    def _():
        carry_ref[...] = jnp.zeros_like(carry_ref)
        r = lax.broadcasted_iota(I32, (OUT_TM, OUT_TM), 0)
        c = lax.broadcasted_iota(I32, (OUT_TM, OUT_TM), 1)
        tri_ref[...] = (r < c).astype(BF16)

    w = w_ref[...].astype(BF16)
    acc = (jnp.dot(a_ref[...], w[:ATTN_WIDTH], preferred_element_type=F32)
           + jnp.dot(c_ref[...], w[ATTN_WIDTH:], preferred_element_type=F32)
           + x_ref[...])
    h_ref[...] = acc
    hacc_ref[n] = acc

    @pl.when(n == OUT_NCH - 1)
    def _():
        ss = jnp.zeros((OUT_TM, 1), F32)
        for j in range(OUT_NCH):
            hj = hacc_ref[j]
            ss = ss + jnp.sum(hj * hj, axis=-1, keepdims=True)
        inv = lax.rsqrt(ss * (1.0 / D_MODEL) + NORM_EPS)
        hn = []
        logits = jnp.zeros((N_EXPERTS, OUT_TM), F32)
        for j in range(OUT_NCH):
            hj = (hacc_ref[j] * inv * fw_ref[:, j * OUT_TN:(j + 1) * OUT_TN]).astype(BF16)
            hn.append(hj)
            logits = logits + lax.dot_general(
                rwt_ref[:, j * OUT_TN:(j + 1) * OUT_TN], hj,
                (((1,), (1,)), ((), ())), preferred_element_type=F32)
        logits = logits + rb_ref[:, 0:1]
        half_chunks = OUT_NCH // 2
        for j in range(half_chunks):
            lo = _bf16_bits(hn[j].astype(F32)) >> 16
            hi = _bf16_bits(hn[j + half_chunks].astype(F32))
            hp_ref[:, j * OUT_TN:(j + 1) * OUT_TN] = lo | hi

        eio = lax.broadcasted_iota(I32, (N_EXPERTS, OUT_TM), 0)
        work = logits
        tops, idxs, hots = [], [], []
        for _ in range(TOP_K):
            mx = jnp.max(work, axis=0, keepdims=True)
            idx = jnp.min(jnp.where(work == mx, eio, N_EXPERTS), axis=0, keepdims=True)
            hot = eio == idx
            tops.append(mx)
            idxs.append(idx)
            hots.append(hot)
            work = jnp.where(hot, -jnp.inf, work)
        exps = [jnp.exp(t - tops[0]) for t in tops]
        den = exps[0] + exps[1] + exps[2] + exps[3]
        multi = jnp.zeros((N_EXPERTS, OUT_TM), F32)
        for hot in hots:
            multi = multi + hot.astype(F32)
        carry = carry_ref[:, 0:1]
        before = carry + jnp.dot(multi.astype(BF16), tri_ref[...], preferred_element_type=F32)
        zero_i = jnp.zeros((1, OUT_TM), I32)
        zero_f = jnp.zeros((1, OUT_TM), F32)
        for k in range(TOP_K):
            ids_ref[k:k + 1, :] = idxs[k]
            gates_ref[k:k + 1, :] = exps[k] / den
            rank = jnp.sum(jnp.where(hots[k], before, 0.0), axis=0, keepdims=True)
            ranks_ref[k:k + 1, :] = rank.astype(I32)
            ids_ref[TOP_K + k:TOP_K + k + 1, :] = zero_i
            gates_ref[TOP_K + k:TOP_K + k + 1, :] = zero_f
            ranks_ref[TOP_K + k:TOP_K + k + 1, :] = zero_i
        new_carry = carry + jnp.sum(multi, axis=1, keepdims=True)
        full = jnp.broadcast_to(new_carry, (N_EXPERTS, V7X_LANES))
        carry_ref[...] = full
        cnt_ref[...] = full


def _outproj_router(attn_n, conv_n, w_out, x2d, ffn_norm_w, router_wt, router_b_col):
    grid = (SEQ // OUT_TM, OUT_NCH)
    meta = lambda m, n: (0, m)
    return pl.pallas_call(
        _outproj_body,
        grid=grid,
        in_specs=[
            pl.BlockSpec((OUT_TM, ATTN_WIDTH), lambda m, n: (m, 0)),
            pl.BlockSpec((OUT_TM, CONV_WIDTH), lambda m, n: (m, 0)),
            pl.BlockSpec((D_MODEL, OUT_TN), lambda m, n: (0, n)),
            pl.BlockSpec((OUT_TM, OUT_TN), lambda m, n: (m, n)),
            pl.BlockSpec((1, D_MODEL), lambda m, n: (0, 0)),
            pl.BlockSpec((N_EXPERTS, D_MODEL), lambda m, n: (0, 0)),
            pl.BlockSpec((N_EXPERTS, V7X_LANES), lambda m, n: (0, 0)),
        ],
        out_specs=[
            pl.BlockSpec((OUT_TM, OUT_TN), lambda m, n: (m, n)),
            pl.BlockSpec((OUT_TM, HALF), lambda m, n: (m, 0)),
            pl.BlockSpec((2 * TOP_K, OUT_TM), meta),
            pl.BlockSpec((2 * TOP_K, OUT_TM), meta),
            pl.BlockSpec((2 * TOP_K, OUT_TM), meta),
            pl.BlockSpec((N_EXPERTS, V7X_LANES), lambda m, n: (0, 0)),
        ],
        out_shape=[
            jax.ShapeDtypeStruct((SEQ, D_MODEL), F32),
            jax.ShapeDtypeStruct((SEQ, HALF), U32),
            jax.ShapeDtypeStruct((2 * TOP_K, SEQ), I32),
            jax.ShapeDtypeStruct((2 * TOP_K, SEQ), F32),
            jax.ShapeDtypeStruct((2 * TOP_K, SEQ), I32),
            jax.ShapeDtypeStruct((N_EXPERTS, V7X_LANES), F32),
        ],
        scratch_shapes=[
            pltpu.VMEM((OUT_NCH, OUT_TM, OUT_TN), F32),
            pltpu.VMEM((N_EXPERTS, V7X_LANES), F32),
            pltpu.VMEM((OUT_TM, OUT_TM), BF16),
        ],
        compiler_params=_cparams(("arbitrary", "arbitrary"), 56),
        name="outproj_router",
    )(attn_n, conv_n, w_out, x2d, ffn_norm_w, router_wt, router_b_col)


HBM_ROW_TILE = 8
PAD_BITS = (128, 64, 32, 16, 8)


def _dispatch_body(dest_ref, pad_start_ref, pad_len_ref, hp_ref, xs_ref, zbuf_ref, sem, zsem):
    i = pl.program_id(0)

    def row_copy(t, d):
        return pltpu.make_async_copy(hp_ref.at[pl.ds(t, 1)], xs_ref.at[pl.ds(d, 1)], sem)

    def pad_head(e):
        return (-pad_start_ref[e]) & (HBM_ROW_TILE - 1)

    def pad_row_copy(e, j):
        return pltpu.make_async_copy(zbuf_ref.at[pl.ds(0, 1)],
                                     xs_ref.at[pl.ds(pad_start_ref[e] + j, 1)], zsem)

    def pad_piece_copy(e, bit):
        body = pad_len_ref[e] - pad_head(e)
        start = pad_start_ref[e] + pad_head(e) + (body & ~(2 * bit - 1))
        start = pl.multiple_of(start, HBM_ROW_TILE)
        return pltpu.make_async_copy(zbuf_ref.at[pl.ds(0, bit)], xs_ref.at[pl.ds(start, bit)], zsem)

    def for_each_pad(action):
        def per_expert(e, carry):
            for j in range(HBM_ROW_TILE - 1):
                @pl.when(j < pad_head(e))
                def _():
                    action(pad_row_copy(e, j))
            for bit in PAD_BITS:
                @pl.when(((pad_len_ref[e] - pad_head(e)) & bit) != 0)
                def _():
                    action(pad_piece_copy(e, bit))
            return carry
        lax.fori_loop(0, N_EXPERTS, per_expert, 0)

    @pl.when(i == 0)
    def _():
        zbuf_ref[...] = jnp.zeros_like(zbuf_ref)
        for_each_pad(lambda cp: cp.start())
        for_each_pad(lambda cp: cp.wait())

    def start_rows(t, carry):
        tok = i * DISPATCH_TB + t
        for k in range(TOP_K):
            row_copy(t, dest_ref[k * SEQ + tok]).start(priority=k % 2)
        return carry
    lax.fori_loop(0, DISPATCH_TB, start_rows, 0, unroll=8)

    def wait_rows(t, carry):
        tok = i * DISPATCH_TB + t
        for k in range(TOP_K):
            row_copy(t, dest_ref[k * SEQ + tok]).wait()
        return carry
    lax.fori_loop(0, DISPATCH_TB, wait_rows, 0, unroll=8)


def _dispatch(dest_flat, pad_start, pad_len, hp):
    grid_spec = pltpu.PrefetchScalarGridSpec(
        num_scalar_prefetch=3,
        grid=(SEQ // DISPATCH_TB,),
        in_specs=[pl.BlockSpec((DISPATCH_TB, HALF), lambda i, d, ps, pn: (i, 0))],
        out_specs=pl.BlockSpec(memory_space=pl.ANY),
        scratch_shapes=[
            pltpu.VMEM((PAD_BITS[0], HALF), U32),
            pltpu.SemaphoreType.DMA(()),
            pltpu.SemaphoreType.DMA(()),
        ],
    )
    return pl.pallas_call(
        _dispatch_body,
        grid_spec=grid_spec,
        out_shape=jax.ShapeDtypeStruct((N_SLOTS, HALF), U32),
        compiler_params=pltpu.CompilerParams(dimension_semantics=("arbitrary",),
                                             has_side_effects=True),
        name="dispatch",
    )(dest_flat, pad_start, pad_len, hp)


def _unpack_rows(words):
    lo = lax.bitcast_convert_type(words << 16, F32).astype(BF16)
    hi = lax.bitcast_convert_type(words & jnp.uint32(0xFFFF0000), F32).astype(BF16)
    return lo, hi


MOE_PASS_WIDTHS = (4, 2, 1)
assert MOE_TM // MOE_SUB < 2 * MOE_PASS_WIDTHS[0]


def _for_each_pass(nsub, fn):
    for width in MOE_PASS_WIDTHS:
        @pl.when((nsub & width) != 0)
        def _():
            first = (nsub & ~(2 * width - 1)) * MOE_SUB
            fn(pl.multiple_of(first, MOE_SUB), width * MOE_SUB)


def _moe_body(be_ref, bx_ref, ns_ref, nu_ref,
              x_ref, wg_ref, wl_ref, wd_ref, bg_ref, bl_ref, bd_ref,
              o_ref, act_ref, xbf_ref):
    b = pl.program_id(0)
    c = pl.program_id(1)
    valid = b < nu_ref[0]
    nsub = ns_ref[b]

    @pl.when(valid & (c == 0))
    def _():
        def unpack(i, carry):
            rows = pl.ds(pl.multiple_of(i * MOE_SUB, MOE_SUB), MOE_SUB)
            lo, hi = _unpack_rows(x_ref[rows, :])
            xbf_ref[0, rows, :] = lo
            xbf_ref[1, rows, :] = hi
            return carry
        lax.fori_loop(0, nsub, unpack, 0)

    @pl.when(valid & (c < MOE_NCH))
    def _():
        def up(first, n_rows):
            rows = pl.ds(first, n_rows)
            lo = xbf_ref[0, rows, :]
            hi = xbf_ref[1, rows, :]
            g = (jnp.dot(lo, wg_ref[0, :HALF, :].astype(BF16), preferred_element_type=F32)
                 + jnp.dot(hi, wg_ref[0, HALF:, :].astype(BF16), preferred_element_type=F32)
                 + bg_ref[0])
            lin = (jnp.dot(lo, wl_ref[0, :HALF, :].astype(BF16), preferred_element_type=F32)
                   + jnp.dot(hi, wl_ref[0, HALF:, :].astype(BF16), preferred_element_type=F32)
                   + bl_ref[0])
            g = jnp.minimum(g, SWIGLU_LIMIT)
            lin = jnp.clip(lin, -SWIGLU_LIMIT, SWIGLU_LIMIT)
            act = g * jax.nn.sigmoid(SWIGLU_ALPHA * g) * (lin + 1.0)
            act_ref[c, rows, :] = act.astype(BF16)
        _for_each_pass(nsub, up)

    @pl.when(valid & (c >= MOE_NCH))
    def _():
        def down(first, n_rows):
            rows = pl.ds(first, n_rows)
            y = jnp.zeros((n_rows, MOE_CH), F32) + bd_ref[0]
            for k in range(MOE_NCH):
                y = y + jnp.dot(act_ref[k, rows, :],
                                wd_ref[0, k * MOE_CH:(k + 1) * MOE_CH, :].astype(BF16),
                                preferred_element_type=F32)
            o_ref[rows, :] = y
        _for_each_pass(nsub, down)

        def fill(i, carry):
            rows = pl.ds(pl.multiple_of(i * MOE_SUB, MOE_SUB), MOE_SUB)
            o_ref[rows, :] = jnp.zeros((MOE_SUB, MOE_CH), F32)
            return carry
        lax.fori_loop(nsub, MOE_TM // MOE_SUB, fill, 0)


def _moe(blk_expert, blk_x, blk_nsub, n_used, xs, w_gate_up, w_down, b_gate_up, b_down):
    last = MOE_NCH - 1

    def up_chunk(b, c, nu):
        return jnp.where(b < nu[0], jnp.minimum(c, last), last)

    def down_chunk(b, c, nu):
        return jnp.where(b < nu[0], jnp.maximum(c - MOE_NCH, 0), last)

    grid_spec = pltpu.PrefetchScalarGridSpec(
        num_scalar_prefetch=4,
        grid=(n_used[0], 2 * MOE_NCH),
        in_specs=[
            pl.BlockSpec((MOE_TM, HALF), lambda b, c, be, bx, ns, nu: (bx[b], 0)),
            pl.BlockSpec((1, D_MODEL, MOE_CH),
                         lambda b, c, be, bx, ns, nu: (be[b], 0, up_chunk(b, c, nu))),
            pl.BlockSpec((1, D_MODEL, MOE_CH),
                         lambda b, c, be, bx, ns, nu: (be[b], 0, MOE_NCH + up_chunk(b, c, nu))),
            pl.BlockSpec((1, D_FF, MOE_CH),
                         lambda b, c, be, bx, ns, nu: (be[b], 0, down_chunk(b, c, nu))),
            pl.BlockSpec((1, 1, MOE_CH),
                         lambda b, c, be, bx, ns, nu: (be[b], 0, up_chunk(b, c, nu))),
            pl.BlockSpec((1, 1, MOE_CH),
                         lambda b, c, be, bx, ns, nu: (be[b], 0, MOE_NCH + up_chunk(b, c, nu))),
            pl.BlockSpec((1, 1, MOE_CH),
                         lambda b, c, be, bx, ns, nu: (be[b], 0, down_chunk(b, c, nu))),
        ],
        out_specs=pl.BlockSpec((MOE_TM, MOE_CH),
                               lambda b, c, be, bx, ns, nu: (bx[b], down_chunk(b, c, nu))),
        scratch_shapes=[pltpu.VMEM((MOE_NCH, MOE_TM, MOE_CH), BF16),
                        pltpu.VMEM((2, MOE_TM, HALF), BF16)],
    )
    return pl.pallas_call(
        _moe_body,
        grid_spec=grid_spec,
        out_shape=jax.ShapeDtypeStruct((N_SLOTS, D_MODEL), F32),
        compiler_params=_cparams(("arbitrary", "arbitrary"), 60),
        name="moe",
    )(blk_expert, blk_x, blk_nsub, n_used, xs, w_gate_up, w_gate_up, w_down,
      b_gate_up, b_gate_up, b_down)


def _combine_body(dest_ref, gates_ref, h_ref, nw_ref, ys_ref, o_ref, buf_ref, sem):
    i = pl.program_id(0)
    nblk = pl.num_programs(0)

    def row_copy(blk, slot, k, t):
        d = dest_ref[k * SEQ + blk * COMBINE_TB + t]
        return pltpu.make_async_copy(ys_ref.at[pl.ds(d, 1)],
                                     buf_ref.at[slot, k, pl.ds(t, 1)], sem.at[slot])

    def start_block(blk, slot):
        def body(t, carry):
            for k in range(TOP_K):
                row_copy(blk, slot, k, t).start(priority=k % 2)
            return carry
        lax.fori_loop(0, COMBINE_TB, body, 0, unroll=8)

    def wait_block(blk, slot):
        def body(t, carry):
            for k in range(TOP_K):
                row_copy(blk, slot, k, t).wait()
            return carry
        lax.fori_loop(0, COMBINE_TB, body, 0, unroll=8)

    slot = i % 2

    @pl.when(i == 0)
    def _():
        start_block(0, 0)

    @pl.when(i + 1 < nblk)
    def _():
        start_block(i + 1, 1 - slot)

    wait_block(i, slot)

    acc = h_ref[...]
    for k in range(TOP_K):
        acc = acc + gates_ref[:, k:k + 1] * buf_ref[slot, k]
    ms = jnp.mean(acc * acc, axis=-1, keepdims=True)
    o_ref[...] = acc * lax.rsqrt(ms + NORM_EPS) * nw_ref[...]


def _combine(dest_flat, gates_t, h, final_norm_w, ys):
    grid_spec = pltpu.PrefetchScalarGridSpec(
        num_scalar_prefetch=1,
        grid=(SEQ // COMBINE_TB,),
        in_specs=[
            pl.BlockSpec((COMBINE_TB, TOP_K), lambda i, d: (i, 0)),
            pl.BlockSpec((COMBINE_TB, D_MODEL), lambda i, d: (i, 0)),
            pl.BlockSpec((1, D_MODEL), lambda i, d: (0, 0)),
            pl.BlockSpec(memory_space=pl.ANY),
        ],
        out_specs=pl.BlockSpec((COMBINE_TB, D_MODEL), lambda i, d: (i, 0)),
        scratch_shapes=[
            pltpu.VMEM((2, TOP_K, COMBINE_TB, D_MODEL), F32),
            pltpu.SemaphoreType.DMA((2,)),
        ],
    )
    return pl.pallas_call(
        _combine_body,
        grid_spec=grid_spec,
        out_shape=jax.ShapeDtypeStruct((SEQ, D_MODEL), F32),
        compiler_params=_cparams(("arbitrary",), 40),
        name="combine",
    )(dest_flat, gates_t, h, final_norm_w, ys)


def _routing_tables(counts, ids, ranks):
    nblk = (counts + MOE_TM - 1) // MOE_TM
    cum = jnp.cumsum(nblk)
    first_blk = cum - nblk
    n_used = cum[-1]
    first_of = jnp.zeros(ids.shape, I32)
    for e in range(N_EXPERTS):
        first_of = first_of + jnp.where(ids == e, first_blk[e], 0)
    dest = first_of * MOE_TM + ranks
    b = jnp.arange(MOE_NB, dtype=I32)
    valid = b < n_used
    blk_e = jnp.minimum(jnp.searchsorted(cum, b, side="right"), N_EXPERTS - 1).astype(I32)
    blk_e = jnp.where(valid, blk_e, blk_e[n_used - 1])
    rows = jnp.clip(counts[blk_e] - (b - first_blk[blk_e]) * MOE_TM, 0, MOE_TM)
    blk_nsub = jnp.where(valid, (rows + MOE_SUB - 1) // MOE_SUB, 0).astype(I32)
    blk_x = jnp.where(valid, b, n_used - 1).astype(I32)
    pad_start = (first_blk * MOE_TM + counts).astype(I32)
    pad_len = ((-counts) % MOE_SUB).astype(I32)
    return (dest.reshape(-1).astype(I32), blk_e, blk_x, blk_nsub,
            n_used.reshape(1).astype(I32), pad_start, pad_len)


def kernel(x, attn_norm_w, w_in, conv_w, rel_pos_bias, attn_out_norm_w, conv_out_norm_w, w_out,
           ffn_norm_w, router_w, router_b, w_gate_up, b_gate_up, w_down, b_down, final_norm_w):
    bsz, s, d = x.shape
    assert (bsz, s, d) == (1, SEQ, D_MODEL)
    assert attn_norm_w.shape[0] == 1
    x2d = x.reshape(SEQ, D_MODEL)

    proj = _inproj(x2d, attn_norm_w[0].reshape(1, D_MODEL), w_in[0])
    attn_n = _attention(proj, _bias_table(rel_pos_bias[0]), attn_out_norm_w[0].reshape(1, ATTN_WIDTH))
    conv_n = _conv(proj, conv_w[0], conv_out_norm_w[0].reshape(1, CONV_WIDTH))

    router_wt = router_w[0].T.astype(BF16)
    router_b_col = jnp.broadcast_to(router_b[0][:, None], (N_EXPERTS, V7X_LANES))
    h, hp, ids, gates, ranks, cnt = _outproj_router(
        attn_n, conv_n, w_out[0], x2d, ffn_norm_w[0].reshape(1, D_MODEL), router_wt, router_b_col)

    counts = cnt[:, 0].astype(I32)
    dest, blk_e, blk_x, blk_nsub, n_used, pad_start, pad_len = _routing_tables(
        counts, ids[:TOP_K], ranks[:TOP_K])

    xs = _dispatch(dest, pad_start, pad_len, hp)
    ys = _moe(blk_e, blk_x, blk_nsub, n_used, xs, w_gate_up[0], w_down[0],
              b_gate_up[0].reshape(N_EXPERTS, 1, 2 * D_FF), b_down[0].reshape(N_EXPERTS, 1, D_MODEL))
    out = _combine(dest, gates[:TOP_K].T, h, final_norm_w.reshape(1, D_MODEL), ys)
    return out.reshape(bsz, s, d)
```

```python
import functools
import math

import jax
import jax.numpy as jnp
from jax import lax
from jax.experimental import pallas as pl
from jax.experimental.pallas import tpu as pltpu

F32 = jnp.float32
BF16 = jnp.bfloat16
I32 = jnp.int32
U32 = jnp.uint32

D_MODEL = 2048
SEQ = 8192
HEAD_DIM = 128
NA_HEADS = 8
ATTN_WIDTH = NA_HEADS * HEAD_DIM
CONV_WIDTH = D_MODEL - ATTN_WIDTH
CONV_GROUPS = CONV_WIDTH // HEAD_DIM
N_PROJ = 3 * ATTN_WIDTH + 3 * CONV_WIDTH
GRID_W = 64
GRID_ROWS = SEQ // GRID_W
NA_KH = 8
NA_KW = 16
N_EXPERTS = 32
TOP_K = 4
D_FF = D_MODEL
SWIGLU_LIMIT = 7.0
SWIGLU_ALPHA = 1.702
NORM_EPS = 1e-5
NEG_INF = -1e30

V7X_LANES = 128
V7X_VMEM_BYTES = 64 * 1024 * 1024
MIB = 1024 * 1024

INPROJ_TM = 1024
INPROJ_TN = 1024
NORM_ROWS = 256
ATTN_ROWS_PER_STEP = 16
WIN_KEYS = NA_KH * GRID_W
CONV_CHUNK = 1024
OUT_TM = 1024
OUT_TN = 512
OUT_NCH = D_MODEL // OUT_TN
MOE_SUB = 256
MOE_TM = 5 * MOE_SUB
MOE_CH = 512
MOE_NCH = D_FF // MOE_CH
MOE_NB = (SEQ * TOP_K) // MOE_TM + 1 + N_EXPERTS - 1
N_SLOTS = MOE_NB * MOE_TM
HALF = D_MODEL // 2
DISPATCH_TB = 1024
COMBINE_TB = 256


def _cparams(semantics, vmem_mib):
    return pltpu.CompilerParams(dimension_semantics=semantics,
                                vmem_limit_bytes=vmem_mib * MIB)


def _inproj_body(x_ref, nw_ref, w_ref, o_ref, xn_ref):
    @pl.when(pl.program_id(1) == 0)
    def _():
        def chunk(i, carry):
            rows = pl.ds(pl.multiple_of(i * NORM_ROWS, NORM_ROWS), NORM_ROWS)
            x = x_ref[rows, :]
            ms = jnp.mean(x * x, axis=-1, keepdims=True)
            xn_ref[rows, :] = (x * lax.rsqrt(ms + NORM_EPS) * nw_ref[...]).astype(BF16)
            return carry
        lax.fori_loop(0, INPROJ_TM // NORM_ROWS, chunk, 0)

    o_ref[...] = jnp.dot(xn_ref[...], w_ref[...].astype(BF16),
                         preferred_element_type=F32).astype(BF16)


def _inproj(x2d, norm_w, w_in):
    return pl.pallas_call(
        _inproj_body,
        grid=(SEQ // INPROJ_TM, N_PROJ // INPROJ_TN),
        in_specs=[
            pl.BlockSpec((INPROJ_TM, D_MODEL), lambda m, n: (m, 0)),
            pl.BlockSpec((1, D_MODEL), lambda m, n: (0, 0)),
            pl.BlockSpec((D_MODEL, INPROJ_TN), lambda m, n: (0, n)),
        ],
        out_specs=pl.BlockSpec((INPROJ_TM, INPROJ_TN), lambda m, n: (m, n)),
        out_shape=jax.ShapeDtypeStruct((SEQ, N_PROJ), BF16),
        scratch_shapes=[pltpu.VMEM((INPROJ_TM, D_MODEL), BF16)],
        compiler_params=_cparams(("parallel", "arbitrary"), 56),
        name="inproj",
    )(x2d, norm_w, w_in)


def _bias_table(rel_pos_bias):
    cq = jnp.arange(GRID_W)[:, None]
    ck = jnp.arange(GRID_W)[None, :]
    cs = jnp.clip(cq - NA_KW // 2, 0, GRID_W - NA_KW)
    band = (ck >= cs) & (ck < cs + NA_KW)
    dc = jnp.clip(ck - cq, -(NA_KW - 1), NA_KW - 1) + (NA_KW - 1)
    t = jnp.zeros((NA_HEADS, 2 * NA_KH - 1, GRID_W, GRID_W), F32)
    for j in range(2 * NA_KW - 1):
        t = t + jnp.where(dc == j, rel_pos_bias[:, :, j, None, None], 0.0)
    t = jnp.where(band[None, None], t, NEG_INF)
    tb = jnp.stack([t[:, off:off + NA_KH] for off in range(NA_KH)], axis=1)
    return tb.transpose(0, 1, 3, 2, 4).reshape(NA_HEADS, NA_KH, GRID_W, WIN_KEYS).astype(F32)


def _attn_body(q_ref, k_ref, v_ref, tb_ref, nw_ref, o_ref):
    rb = pl.program_id(1)
    scale = 1.0 / math.sqrt(HEAD_DIM)
    for i in range(ATTN_ROWS_PER_STEP):
        r = rb * ATTN_ROWS_PER_STEP + i
        ws = jnp.clip(r - NA_KH // 2, 0, GRID_ROWS - NA_KH)
        off = ws - r + (NA_KH - 1)
        kstart = pl.multiple_of(ws * GRID_W, GRID_W)
        kk = k_ref[pl.ds(kstart, WIN_KEYS), :]
        vv = v_ref[pl.ds(kstart, WIN_KEYS), :]
        q = q_ref[i * GRID_W:(i + 1) * GRID_W, :]
        s = lax.dot_general(q, kk, (((1,), (1,)), ((), ())), preferred_element_type=F32)
        s = s * scale + tb_ref[0, off]
        m = jnp.max(s, axis=-1, keepdims=True)
        p = jnp.exp(s - m)
        l = jnp.sum(p, axis=-1, keepdims=True)
        o = jnp.dot(p.astype(BF16), vv, preferred_element_type=F32) / l
        ms = jnp.mean(o * o, axis=-1, keepdims=True)
        o_ref[i * GRID_W:(i + 1) * GRID_W, :] = (
            o * lax.rsqrt(ms + NORM_EPS) * nw_ref[...]).astype(BF16)


def _attention(proj, bias_tab, attn_out_norm_w):
    rows = ATTN_ROWS_PER_STEP * GRID_W
    return pl.pallas_call(
        _attn_body,
        grid=(NA_HEADS, GRID_ROWS // ATTN_ROWS_PER_STEP),
        in_specs=[
            pl.BlockSpec((rows, HEAD_DIM), lambda h, rb: (rb, h)),
            pl.BlockSpec((SEQ, HEAD_DIM), lambda h, rb: (0, NA_HEADS + h)),
            pl.BlockSpec((SEQ, HEAD_DIM), lambda h, rb: (0, 2 * NA_HEADS + h)),
            pl.BlockSpec((1, NA_KH, GRID_W, WIN_KEYS), lambda h, rb: (h, 0, 0, 0)),
            pl.BlockSpec((1, HEAD_DIM), lambda h, rb: (0, h)),
        ],
        out_specs=pl.BlockSpec((rows, HEAD_DIM), lambda h, rb: (rb, h)),
        out_shape=jax.ShapeDtypeStruct((SEQ, ATTN_WIDTH), BF16),
        compiler_params=_cparams(("parallel", "arbitrary"), 32),
        name="attn",
    )(proj, proj, proj, bias_tab, attn_out_norm_w)


CONV_PAD = 8


def _conv_body(u_ref, b_ref, c_ref, cw_ref, nw_ref, o_ref, z_ref):
    zeros = jnp.zeros((CONV_PAD, HEAD_DIM), F32)
    z_ref[0:CONV_PAD, :] = zeros
    z_ref[CONV_PAD + SEQ:2 * CONV_PAD + SEQ, :] = zeros
    for ch in range(SEQ // CONV_CHUNK):
        lo = ch * CONV_CHUNK
        z_ref[CONV_PAD + lo:CONV_PAD + lo + CONV_CHUNK, :] = (
            c_ref[lo:lo + CONV_CHUNK, :].astype(F32) * u_ref[lo:lo + CONV_CHUNK, :].astype(F32))
    w0 = cw_ref[0:1, :]
    w1 = cw_ref[1:2, :]
    w2 = cw_ref[2:3, :]
    for ch in range(SEQ // CONV_CHUNK):
        lo = ch * CONV_CHUNK
        base = CONV_PAD + lo
        conv = (w0 * z_ref[base - 1:base - 1 + CONV_CHUNK, :]
                + w1 * z_ref[base:base + CONV_CHUNK, :]
                + w2 * z_ref[base + 1:base + 1 + CONV_CHUNK, :])
        y = b_ref[lo:lo + CONV_CHUNK, :].astype(F32) * conv
        ms = jnp.mean(y * y, axis=-1, keepdims=True)
        o_ref[lo:lo + CONV_CHUNK, :] = (y * lax.rsqrt(ms + NORM_EPS) * nw_ref[...]).astype(BF16)


def _conv(proj, conv_w, conv_out_norm_w):
    col0 = 3 * NA_HEADS
    return pl.pallas_call(
        _conv_body,
        grid=(CONV_GROUPS,),
        in_specs=[
            pl.BlockSpec((SEQ, HEAD_DIM), lambda g: (0, col0 + g)),
            pl.BlockSpec((SEQ, HEAD_DIM), lambda g: (0, col0 + CONV_GROUPS + g)),
            pl.BlockSpec((SEQ, HEAD_DIM), lambda g: (0, col0 + 2 * CONV_GROUPS + g)),
            pl.BlockSpec((3, HEAD_DIM), lambda g: (0, g)),
            pl.BlockSpec((1, HEAD_DIM), lambda g: (0, g)),
        ],
        out_specs=pl.BlockSpec((SEQ, HEAD_DIM), lambda g: (0, g)),
        out_shape=jax.ShapeDtypeStruct((SEQ, CONV_WIDTH), BF16),
        scratch_shapes=[pltpu.VMEM((SEQ + 2 * CONV_PAD, HEAD_DIM), F32)],
        compiler_params=_cparams(("arbitrary",), 48),
        name="conv",
    )(proj, proj, proj, conv_w, conv_out_norm_w)


def _bf16_bits(x):
    return lax.bitcast_convert_type(x.astype(BF16).astype(F32), U32) & jnp.uint32(0xFFFF0000)


def _outproj_body(a_ref, c_ref, w_ref, x_ref, fw_ref, rwt_ref, rb_ref,
                  h_ref, hp_ref, ids_ref, gates_ref, ranks_ref, cnt_ref,
                  hacc_ref, carry_ref, tri_ref):
    m = pl.program_id(0)
    n = pl.program_id(1)

    @pl.when((m == 0) & (n == 0))
    def _():
        carry_ref[...] = jnp.zeros_like(carry_ref)
        r = lax.broadcasted_iota(I32, (OUT_TM, OUT_TM), 0)
        c = lax.broadcasted_iota(I32, (OUT_TM, OUT_TM), 1)
        tri_ref[...] = (r < c).astype(BF16)

    w = w_ref[...].astype(BF16)
    acc = (jnp.dot(a_ref[...], w[:ATTN_WIDTH], preferred_element_type=F32)
           + jnp.dot(c_ref[...], w[ATTN_WIDTH:], preferred_element_type=F32)
           + x_ref[...])
    h_ref[...] = acc
    hacc_ref[n] = acc

    @pl.when(n == OUT_NCH - 1)
    def _():
        ss = jnp.zeros((OUT_TM, 1), F32)
        for j in range(OUT_NCH):
            hj = hacc_ref[j]
            ss = ss + jnp.sum(hj * hj, axis=-1, keepdims=True)
        inv = lax.rsqrt(ss * (1.0 / D_MODEL) + NORM_EPS)
        hn = []
        logits = jnp.zeros((N_EXPERTS, OUT_TM), F32)
        for j in range(OUT_NCH):
            hj = (hacc_ref[j] * inv * fw_ref[:, j * OUT_TN:(j + 1) * OUT_TN]).astype(BF16)
            hn.append(hj)
            logits = logits + lax.dot_general(
                rwt_ref[:, j * OUT_TN:(j + 1) * OUT_TN], hj,
                (((1,), (1,)), ((), ())), preferred_element_type=F32)
        logits = logits + rb_ref[:, 0:1]
        half_chunks = OUT_NCH // 2
        for j in range(half_chunks):
            lo = _bf16_bits(hn[j].astype(F32)) >> 16
            hi = _bf16_bits(hn[j + half_chunks].astype(F32))
            hp_ref[:, j * OUT_TN:(j + 1) * OUT_TN] = lo | hi

        eio = lax.broadcasted_iota(I32, (N_EXPERTS, OUT_TM), 0)
        work = logits
        tops, idxs, hots = [], [], []
        for _ in range(TOP_K):
            mx = jnp.max(work, axis=0, keepdims=True)
            idx = jnp.min(jnp.where(work == mx, eio, N_EXPERTS), axis=0, keepdims=True)
            hot = eio == idx
            tops.append(mx)
            idxs.append(idx)
            hots.append(hot)
            work = jnp.where(hot, -jnp.inf, work)
        exps = [jnp.exp(t - tops[0]) for t in tops]
        den = exps[0] + exps[1] + exps[2] + exps[3]
        multi = jnp.zeros((N_EXPERTS, OUT_TM), F32)
        for hot in hots:
            multi = multi + hot.astype(F32)
        carry = carry_ref[:, 0:1]
        before = carry + jnp.dot(multi.astype(BF16), tri_ref[...], preferred_element_type=F32)
        zero_i = jnp.zeros((1, OUT_TM), I32)
        zero_f = jnp.zeros((1, OUT_TM), F32)
        for k in range(TOP_K):
            ids_ref[k:k + 1, :] = idxs[k]
            gates_ref[k:k + 1, :] = exps[k] / den
            rank = jnp.sum(jnp.where(hots[k], before, 0.0), axis=0, keepdims=True)
            ranks_ref[k:k + 1, :] = rank.astype(I32)
            ids_ref[TOP_K + k:TOP_K + k + 1, :] = zero_i
            gates_ref[TOP_K + k:TOP_K + k + 1, :] = zero_f
            ranks_ref[TOP_K + k:TOP_K + k + 1, :] = zero_i
        new_carry = carry + jnp.sum(multi, axis=1, keepdims=True)
        full = jnp.broadcast_to(new_carry, (N_EXPERTS, V7X_LANES))
        carry_ref[...] = full
        cnt_ref[...] = full


def _outproj_router(attn_n, conv_n, w_out, x2d, ffn_norm_w, router_wt, router_b_col):
    grid = (SEQ // OUT_TM, OUT_NCH)
    meta = lambda m, n: (0, m)
    return pl.pallas_call(
        _outproj_body,
        grid=grid,
        in_specs=[
            pl.BlockSpec((OUT_TM, ATTN_WIDTH), lambda m, n: (m, 0)),
            pl.BlockSpec((OUT_TM, CONV_WIDTH), lambda m, n: (m, 0)),
            pl.BlockSpec((D_MODEL, OUT_TN), lambda m, n: (0, n)),
            pl.BlockSpec((OUT_TM, OUT_TN), lambda m, n: (m, n)),
            pl.BlockSpec((1, D_MODEL), lambda m, n: (0, 0)),
            pl.BlockSpec((N_EXPERTS, D_MODEL), lambda m, n: (0, 0)),
            pl.BlockSpec((N_EXPERTS, V7X_LANES), lambda m, n: (0, 0)),
        ],
        out_specs=[
            pl.BlockSpec((OUT_TM, OUT_TN), lambda m, n: (m, n)),
            pl.BlockSpec((OUT_TM, HALF), lambda m, n: (m, 0)),
            pl.BlockSpec((2 * TOP_K, OUT_TM), meta),
            pl.BlockSpec((2 * TOP_K, OUT_TM), meta),
            pl.BlockSpec((2 * TOP_K, OUT_TM), meta),
            pl.BlockSpec((N_EXPERTS, V7X_LANES), lambda m, n: (0, 0)),
        ],
        out_shape=[
            jax.ShapeDtypeStruct((SEQ, D_MODEL), F32),
            jax.ShapeDtypeStruct((SEQ, HALF), U32),
            jax.ShapeDtypeStruct((2 * TOP_K, SEQ), I32),
            jax.ShapeDtypeStruct((2 * TOP_K, SEQ), F32),
            jax.ShapeDtypeStruct((2 * TOP_K, SEQ), I32),
            jax.ShapeDtypeStruct((N_EXPERTS, V7X_LANES), F32),
        ],
        scratch_shapes=[
            pltpu.VMEM((OUT_NCH, OUT_TM, OUT_TN), F32),
            pltpu.VMEM((N_EXPERTS, V7X_LANES), F32),
            pltpu.VMEM((OUT_TM, OUT_TM), BF16),
        ],
        compiler_params=_cparams(("arbitrary", "arbitrary"), 56),
        name="outproj_router",
    )(attn_n, conv_n, w_out, x2d, ffn_norm_w, router_wt, router_b_col)


HBM_ROW_TILE = 8
PAD_BITS = (128, 64, 32, 16, 8)


def _dispatch_body(dest_ref, pad_start_ref, pad_len_ref, hp_ref, xs_ref, zbuf_ref, sem, zsem):
    i = pl.program_id(0)

    def row_copy(t, d):
        return pltpu.make_async_copy(hp_ref.at[pl.ds(t, 1)], xs_ref.at[pl.ds(d, 1)], sem)

    def pad_head(e):
        return (-pad_start_ref[e]) & (HBM_ROW_TILE - 1)

    def pad_row_copy(e, j):
        return pltpu.make_async_copy(zbuf_ref.at[pl.ds(0, 1)],
                                     xs_ref.at[pl.ds(pad_start_ref[e] + j, 1)], zsem)

    def pad_piece_copy(e, bit):
        body = pad_len_ref[e] - pad_head(e)
        start = pad_start_ref[e] + pad_head(e) + (body & ~(2 * bit - 1))
        start = pl.multiple_of(start, HBM_ROW_TILE)
        return pltpu.make_async_copy(zbuf_ref.at[pl.ds(0, bit)], xs_ref.at[pl.ds(start, bit)], zsem)

    def for_each_pad(action):
        def per_expert(e, carry):
            for j in range(HBM_ROW_TILE - 1):
                @pl.when(j < pad_head(e))
                def _():
                    action(pad_row_copy(e, j))
            for bit in PAD_BITS:
                @pl.when(((pad_len_ref[e] - pad_head(e)) & bit) != 0)
                def _():
                    action(pad_piece_copy(e, bit))
            return carry
        lax.fori_loop(0, N_EXPERTS, per_expert, 0)

    @pl.when(i == 0)
    def _():
        zbuf_ref[...] = jnp.zeros_like(zbuf_ref)
        for_each_pad(lambda cp: cp.start())
        for_each_pad(lambda cp: cp.wait())

    def start_rows(t, carry):
        tok = i * DISPATCH_TB + t
        for k in range(TOP_K):
            row_copy(t, dest_ref[k * SEQ + tok]).start(priority=k % 2)
        return carry
    lax.fori_loop(0, DISPATCH_TB, start_rows, 0, unroll=16)

    def wait_rows(t, carry):
        tok = i * DISPATCH_TB + t
        for k in range(TOP_K):
            row_copy(t, dest_ref[k * SEQ + tok]).wait()
        return carry
    lax.fori_loop(0, DISPATCH_TB, wait_rows, 0, unroll=16)


def _dispatch(dest_flat, pad_start, pad_len, hp):
    grid_spec = pltpu.PrefetchScalarGridSpec(
        num_scalar_prefetch=3,
        grid=(SEQ // DISPATCH_TB,),
        in_specs=[pl.BlockSpec((DISPATCH_TB, HALF), lambda i, d, ps, pn: (i, 0))],
        out_specs=pl.BlockSpec(memory_space=pl.ANY),
        scratch_shapes=[
            pltpu.VMEM((PAD_BITS[0], HALF), U32),
            pltpu.SemaphoreType.DMA(()),
            pltpu.SemaphoreType.DMA(()),
        ],
    )
    return pl.pallas_call(
        _dispatch_body,
        grid_spec=grid_spec,
        out_shape=jax.ShapeDtypeStruct((N_SLOTS, HALF), U32),
        compiler_params=pltpu.CompilerParams(dimension_semantics=("arbitrary",),
                                             has_side_effects=True),
        name="dispatch",
    )(dest_flat, pad_start, pad_len, hp)


def _unpack_rows(words):
    lo = lax.bitcast_convert_type(words << 16, F32).astype(BF16)
    hi = lax.bitcast_convert_type(words & jnp.uint32(0xFFFF0000), F32).astype(BF16)
    return lo, hi


MOE_PASS_WIDTHS = (4, 2, 1)
assert MOE_TM // MOE_SUB < 2 * MOE_PASS_WIDTHS[0]


def _for_each_pass(nsub, fn):
    for width in MOE_PASS_WIDTHS:
        @pl.when((nsub & width) != 0)
        def _():
            first = (nsub & ~(2 * width - 1)) * MOE_SUB
            fn(pl.multiple_of(first, MOE_SUB), width * MOE_SUB)


def _moe_body(be_ref, bx_ref, ns_ref,
              x_ref, wg_ref, wl_ref, wd_ref, bg_ref, bl_ref, bd_ref,
              o_ref, act_ref):
    c = pl.program_id(1)
    nsub = ns_ref[pl.program_id(0)]

    @pl.when(c < MOE_NCH)
    def _():
        def up(first, n_rows):
            rows = pl.ds(first, n_rows)
            lo, hi = _unpack_rows(x_ref[rows, :])
            g = (jnp.dot(lo, wg_ref[0, :HALF, :].astype(BF16), preferred_element_type=F32)
                 + jnp.dot(hi, wg_ref[0, HALF:, :].astype(BF16), preferred_element_type=F32)
                 + bg_ref[0])
            lin = (jnp.dot(lo, wl_ref[0, :HALF, :].astype(BF16), preferred_element_type=F32)
                   + jnp.dot(hi, wl_ref[0, HALF:, :].astype(BF16), preferred_element_type=F32)
                   + bl_ref[0])
            g = jnp.minimum(g, SWIGLU_LIMIT)
            lin = jnp.clip(lin, -SWIGLU_LIMIT, SWIGLU_LIMIT)
            act = g * jax.nn.sigmoid(SWIGLU_ALPHA * g) * (lin + 1.0)
            act_ref[c, rows, :] = act.astype(BF16)
        _for_each_pass(nsub, up)

    @pl.when(c >= MOE_NCH)
    def _():
        def down(first, n_rows):
            rows = pl.ds(first, n_rows)
            y = jnp.zeros((n_rows, MOE_CH), F32) + bd_ref[0]
            for k in range(MOE_NCH):
                y = y + jnp.dot(act_ref[k, rows, :],
                                wd_ref[0, k * MOE_CH:(k + 1) * MOE_CH, :].astype(BF16),
                                preferred_element_type=F32)
            o_ref[rows, :] = y
        _for_each_pass(nsub, down)

        def fill(i, carry):
            rows = pl.ds(pl.multiple_of(i * MOE_SUB, MOE_SUB), MOE_SUB)
            o_ref[rows, :] = jnp.zeros((MOE_SUB, MOE_CH), F32)
            return carry
        lax.fori_loop(nsub, MOE_TM // MOE_SUB, fill, 0)


def _moe(blk_expert, blk_x, blk_nsub, n_used, xs, w_gate_up, w_down, b_gate_up, b_down):
    def up_chunk(c):
        return jnp.minimum(c, MOE_NCH - 1)

    def down_chunk(c):
        return jnp.maximum(c - MOE_NCH, 0)

    grid_spec = pltpu.PrefetchScalarGridSpec(
        num_scalar_prefetch=3,
        grid=(n_used[0], 2 * MOE_NCH),
        in_specs=[
            pl.BlockSpec((MOE_TM, HALF), lambda b, c, be, bx, ns: (bx[b], 0)),
            pl.BlockSpec((1, D_MODEL, MOE_CH),
                         lambda b, c, be, bx, ns: (be[b], 0, up_chunk(c))),
            pl.BlockSpec((1, D_MODEL, MOE_CH),
                         lambda b, c, be, bx, ns: (be[b], 0, MOE_NCH + up_chunk(c))),
            pl.BlockSpec((1, D_FF, MOE_CH),
                         lambda b, c, be, bx, ns: (be[b], 0, down_chunk(c))),
            pl.BlockSpec((1, 1, MOE_CH),
                         lambda b, c, be, bx, ns: (be[b], 0, up_chunk(c))),
            pl.BlockSpec((1, 1, MOE_CH),
                         lambda b, c, be, bx, ns: (be[b], 0, MOE_NCH + up_chunk(c))),
            pl.BlockSpec((1, 1, MOE_CH),
                         lambda b, c, be, bx, ns: (be[b], 0, down_chunk(c))),
        ],
        out_specs=pl.BlockSpec((MOE_TM, MOE_CH),
                               lambda b, c, be, bx, ns: (bx[b], down_chunk(c))),
        scratch_shapes=[pltpu.VMEM((MOE_NCH, MOE_TM, MOE_CH), BF16)],
    )
    return pl.pallas_call(
        _moe_body,
        grid_spec=grid_spec,
        out_shape=jax.ShapeDtypeStruct((N_SLOTS, D_MODEL), F32),
        compiler_params=_cparams(("arbitrary", "arbitrary"), 60),
        name="moe",
    )(blk_expert, blk_x, blk_nsub, xs, w_gate_up, w_gate_up, w_down,
      b_gate_up, b_gate_up, b_down)


def _combine_body(dest_ref, gates_ref, h_ref, nw_ref, ys_ref, o_ref, buf_ref, sem):
    i = pl.program_id(0)
    nblk = pl.num_programs(0)

    def row_copy(blk, slot, k, t):
        d = dest_ref[k * SEQ + blk * COMBINE_TB + t]
        return pltpu.make_async_copy(ys_ref.at[pl.ds(d, 1)],
                                     buf_ref.at[slot, k, pl.ds(t, 1)], sem.at[slot])

    def start_block(blk, slot):
        def body(t, carry):
            for k in range(TOP_K):
                row_copy(blk, slot, k, t).start(priority=k % 2)
            return carry
        lax.fori_loop(0, COMBINE_TB, body, 0, unroll=16)

    def wait_block(blk, slot):
        def body(t, carry):
            for k in range(TOP_K):
                row_copy(blk, slot, k, t).wait()
            return carry
        lax.fori_loop(0, COMBINE_TB, body, 0, unroll=16)

    slot = i % 2

    @pl.when(i == 0)
    def _():
        start_block(0, 0)

    @pl.when(i + 1 < nblk)
    def _():
        start_block(i + 1, 1 - slot)

    wait_block(i, slot)

    acc = h_ref[...]
    for k in range(TOP_K):
        acc = acc + gates_ref[:, k:k + 1] * buf_ref[slot, k]
    ms = jnp.mean(acc * acc, axis=-1, keepdims=True)
    o_ref[...] = acc * lax.rsqrt(ms + NORM_EPS) * nw_ref[...]


def _combine(dest_flat, gates_t, h, final_norm_w, ys):
    grid_spec = pltpu.PrefetchScalarGridSpec(
        num_scalar_prefetch=1,
        grid=(SEQ // COMBINE_TB,),
        in_specs=[
            pl.BlockSpec((COMBINE_TB, TOP_K), lambda i, d: (i, 0)),
            pl.BlockSpec((COMBINE_TB, D_MODEL), lambda i, d: (i, 0)),
            pl.BlockSpec((1, D_MODEL), lambda i, d: (0, 0)),
            pl.BlockSpec(memory_space=pl.ANY),
        ],
        out_specs=pl.BlockSpec((COMBINE_TB, D_MODEL), lambda i, d: (i, 0)),
        scratch_shapes=[
            pltpu.VMEM((2, TOP_K, COMBINE_TB, D_MODEL), F32),
            pltpu.SemaphoreType.DMA((2,)),
        ],
    )
    return pl.pallas_call(
        _combine_body,
        grid_spec=grid_spec,
        out_shape=jax.ShapeDtypeStruct((SEQ, D_MODEL), F32),
        compiler_params=_cparams(("arbitrary",), 40),
        name="combine",
    )(dest_flat, gates_t, h, final_norm_w, ys)


def _routing_tables(counts, ids, ranks):
    nblk = (counts + MOE_TM - 1) // MOE_TM
    cum = jnp.cumsum(nblk)
    first_blk = cum - nblk
    n_used = cum[-1]
    first_of = jnp.zeros(ids.shape, I32)
    for e in range(N_EXPERTS):
        first_of = first_of + jnp.where(ids == e, first_blk[e], 0)
    dest = first_of * MOE_TM + ranks
    b = jnp.arange(MOE_NB, dtype=I32)
    valid = b < n_used
    blk_e = jnp.minimum(jnp.searchsorted(cum, b, side="right"), N_EXPERTS - 1).astype(I32)
    blk_e = jnp.where(valid, blk_e, blk_e[n_used - 1])
    rows = jnp.clip(counts[blk_e] - (b - first_blk[blk_e]) * MOE_TM, 0, MOE_TM)
    blk_nsub = jnp.where(valid, (rows + MOE_SUB - 1) // MOE_SUB, 0).astype(I32)
    blk_x = jnp.where(valid, b, n_used - 1).astype(I32)
    pad_start = (first_blk * MOE_TM + counts).astype(I32)
    pad_len = ((-counts) % MOE_SUB).astype(I32)
    return (dest.reshape(-1).astype(I32), blk_e, blk_x, blk_nsub,
            n_used.reshape(1).astype(I32), pad_start, pad_len)


def kernel(x, attn_norm_w, w_in, conv_w, rel_pos_bias, attn_out_norm_w, conv_out_norm_w, w_out,
           ffn_norm_w, router_w, router_b, w_gate_up, b_gate_up, w_down, b_down, final_norm_w):
    bsz, s, d = x.shape
    assert (bsz, s, d) == (1, SEQ, D_MODEL)
    assert attn_norm_w.shape[0] == 1
    x2d = x.reshape(SEQ, D_MODEL)

    proj = _inproj(x2d, attn_norm_w[0].reshape(1, D_MODEL), w_in[0])
    attn_n = _attention(proj, _bias_table(rel_pos_bias[0]), attn_out_norm_w[0].reshape(1, ATTN_WIDTH))
    conv_n = _conv(proj, conv_w[0], conv_out_norm_w[0].reshape(1, CONV_WIDTH))

    router_wt = router_w[0].T.astype(BF16)
    router_b_col = jnp.broadcast_to(router_b[0][:, None], (N_EXPERTS, V7X_LANES))
    h, hp, ids, gates, ranks, cnt = _outproj_router(
        attn_n, conv_n, w_out[0], x2d, ffn_norm_w[0].reshape(1, D_MODEL), router_wt, router_b_col)

    counts = cnt[:, 0].astype(I32)
    dest, blk_e, blk_x, blk_nsub, n_used, pad_start, pad_len = _routing_tables(
        counts, ids[:TOP_K], ranks[:TOP_K])

    xs = _dispatch(dest, pad_start, pad_len, hp)
    ys = _moe(blk_e, blk_x, blk_nsub, n_used, xs, w_gate_up[0], w_down[0],
              b_gate_up[0].reshape(N_EXPERTS, 1, 2 * D_FF), b_down[0].reshape(N_EXPERTS, 1, D_MODEL))
    out = _combine(dest, gates[:TOP_K].T, h, final_norm_w.reshape(1, D_MODEL), ys)
    return out.reshape(bsz, s, d)
```
